```python
import jax, jax.numpy as jnp
from jax import lax
import numpy as np

D_MODEL = 2048
BATCH = 8
SEQ = 2048
DEPTH = 2

GRID_W = 64
CTX_LEN = 256
N_MIXERS = 2
N_POOL_LAYERS = (DEPTH + N_MIXERS - 1) // N_MIXERS
N_MLA_LAYERS = DEPTH // N_MIXERS
POOL_WIDTH = D_MODEL
POOL_GROUPS = 4
POOL_GROUP_DIM = POOL_WIDTH // POOL_GROUPS
POOL_WINDOWS = (2, 4, 8, 16)
MLA_HEADS = 16
Q_LORA_RANK = 512
KV_LORA_RANK = 512
QK_NOPE_DIM = 128
QK_ROPE_DIM = 64
V_HEAD_DIM = 128
MLA_WIDTH = MLA_HEADS * V_HEAD_DIM
MLA_IN_WIDTH = Q_LORA_RANK + KV_LORA_RANK + QK_ROPE_DIM + MLA_WIDTH
MLA_SCALE = (QK_NOPE_DIM + QK_ROPE_DIM) ** -0.5
ROPE_BASE = 10000.0
Q_BLOCK = 128
NORM_EPS = 1e-6

kernel_name = "hybrid_pool_mla_dit_prefix"


def rms_norm(x, g):
    xf = x.astype(jnp.float32)
    y = xf * lax.rsqrt(jnp.mean(xf * xf, axis=-1, keepdims=True) + NORM_EPS)
    return (y * g.astype(jnp.float32)).astype(x.dtype)


def centred_multiscale_pool(u):
    L = u.shape[1]
    uf = u.astype(jnp.float32)
    cs = jnp.concatenate([jnp.zeros_like(uf[:, :1]), jnp.cumsum(uf, axis=1)], axis=1)
    win = jnp.array(POOL_WINDOWS, dtype=jnp.int32)
    left = win // 2
    right = win - 1 - left
    t = jnp.arange(L, dtype=jnp.int32)[:, None]
    hi = jnp.minimum(t + right + 1, L)
    lo = jnp.maximum(t - left, 0)
    g = jnp.arange(POOL_GROUPS, dtype=jnp.int32)[None, :]
    s = cs[:, hi, g] - cs[:, lo, g]
    mean = s / (hi - lo).astype(jnp.float32)[None, :, :, None]
    return (mean - uf).astype(u.dtype)


def pool_branch(h, w_in, w_grp, b_grp, scale, w_out):
    B, L, _ = h.shape
    u, gate = jnp.split(h @ w_in, 2, axis=-1)
    p = centred_multiscale_pool(u.reshape(B, L, POOL_GROUPS, POOL_GROUP_DIM))
    p = jnp.einsum('blgc,gcd->blgd', p, w_grp) + b_grp
    y = p.reshape(B, L, POOL_WIDTH) * scale
    return (y * jax.nn.silu(gate)) @ w_out


def axial_rope_tables(L):
    rows = L // GRID_W
    row = jnp.repeat(jnp.arange(rows, dtype=jnp.float32), GRID_W)
    col = jnp.tile(jnp.arange(GRID_W, dtype=jnp.float32), rows)
    n = QK_ROPE_DIM // 4
    freqs = ROPE_BASE ** (-jnp.arange(n, dtype=jnp.float32) / n)
    ang = jnp.stack([row[:, None] * freqs, col[:, None] * freqs], axis=1)
    return jnp.cos(ang), jnp.sin(ang)


def apply_axial_rope(x, cos, sin):
    xs = x.reshape(x.shape[:-1] + (2, 2, QK_ROPE_DIM // 4))
    x1, x2 = xs[..., 0, :], xs[..., 1, :]
    cos = cos.astype(x.dtype)
    sin = sin.astype(x.dtype)
    out = jnp.stack([x1 * cos - x2 * sin, x1 * sin + x2 * cos], axis=-2)
    return out.reshape(x.shape)


def mla_project(h, w_in, q_norm, kv_norm, w_uq, w_ukv, rope):
    B, L, _ = h.shape
    proj = h @ w_in
    c_q, c_kv, k_r, gate = jnp.split(
        proj, [Q_LORA_RANK, Q_LORA_RANK + KV_LORA_RANK, Q_LORA_RANK + KV_LORA_RANK + QK_ROPE_DIM], axis=-1)
    q = (rms_norm(c_q, q_norm) @ w_uq).reshape(B, L, MLA_HEADS, QK_NOPE_DIM + QK_ROPE_DIM)
    q_nope, q_rope = q[..., :QK_NOPE_DIM], q[..., QK_NOPE_DIM:]
    kv = (rms_norm(c_kv, kv_norm) @ w_ukv).reshape(B, L, MLA_HEADS, QK_NOPE_DIM + V_HEAD_DIM)
    k_nope, v = kv[..., :QK_NOPE_DIM], kv[..., QK_NOPE_DIM:]
    if rope is not None:
        cos, sin = rope
        q_rope = apply_axial_rope(q_rope, cos[:, None], sin[:, None])
        k_r = apply_axial_rope(k_r, cos, sin)
    return q_nope, q_rope, k_nope, k_r, v, gate


def mla_attend(q_nope, q_rope, k_nope, k_rope, v):
    s = (jnp.einsum('bqhd,bkhd->bhqk', q_nope, k_nope)
         + jnp.einsum('bqhr,bkr->bhqk', q_rope, k_rope)).astype(jnp.float32) * MLA_SCALE
    p = jax.nn.softmax(s, axis=-1).astype(v.dtype)
    return jnp.einsum('bhqk,bkhd->bqhd', p, v)


def blocked_mla_attend(q_nope, q_rope, k_nope, k_rope, v):
    B, L, H, _ = q_nope.shape
    nb = L // Q_BLOCK
    qn = q_nope.reshape(B, nb, Q_BLOCK, H, QK_NOPE_DIM).transpose(1, 0, 2, 3, 4)
    qr = q_rope.reshape(B, nb, Q_BLOCK, H, QK_ROPE_DIM).transpose(1, 0, 2, 3, 4)
    out = lax.map(lambda qb: mla_attend(qb[0], qb[1], k_nope, k_rope, v), (qn, qr))
    return out.transpose(1, 0, 2, 3, 4).reshape(B, L, H * V_HEAD_DIM)


def mla_branch(h_lat, h_ctx, w_in, q_norm, kv_norm, w_uq, w_ukv, w_out, need_ctx_out):
    L = h_lat.shape[1]
    rope = axial_rope_tables(L)
    qn_l, qr_l, kn_l, kr_l, v_l, g_l = mla_project(h_lat, w_in, q_norm, kv_norm, w_uq, w_ukv, rope)
    qn_c, qr_c, kn_c, kr_c, v_c, g_c = mla_project(h_ctx, w_in, q_norm, kv_norm, w_uq, w_ukv, None)
    kn = jnp.concatenate([kn_c, kn_l], axis=1)
    kr = jnp.concatenate([kr_c, kr_l], axis=1)
    v = jnp.concatenate([v_c, v_l], axis=1)
    o_lat = blocked_mla_attend(qn_l, qr_l, kn, kr, v)
    y_lat = (o_lat * jax.nn.silu(g_l)) @ w_out
    if not need_ctx_out:
        return y_lat, None
    B, Lc = h_ctx.shape[:2]
    o_ctx = mla_attend(qn_c, qr_c, kn_c, kr_c, v_c).reshape(B, Lc, MLA_WIDTH)
    y_ctx = (o_ctx * jax.nn.silu(g_c)) @ w_out
    return y_lat, y_ctx


def setup_inputs(seed: int = 0) -> dict:
    key = jax.random.key(seed)
    ks = jax.random.split(key, 20)
    nrm = jax.random.normal
    f32 = jnp.float32
    return {
        "x": nrm(ks[0], (BATCH, SEQ, D_MODEL), f32),
        "c": nrm(ks[1], (BATCH, D_MODEL), f32),
        "ctx": nrm(ks[2], (BATCH, CTX_LEN, D_MODEL), f32),
        "c_ctx": nrm(ks[3], (D_MODEL,), f32),
        "ada_w": nrm(ks[4], (DEPTH, D_MODEL, 3 * D_MODEL), f32) * D_MODEL ** -0.5,
        "ada_b": nrm(ks[5], (DEPTH, 3 * D_MODEL), f32) * 0.01,
        "pre_norm": 1.0 + 0.1 * nrm(ks[6], (DEPTH, D_MODEL), f32),
        "post_norm": 1.0 + 0.1 * nrm(ks[7], (DEPTH, D_MODEL), f32),
        "pool_w_in": nrm(ks[8], (N_POOL_LAYERS, D_MODEL, 2 * POOL_WIDTH), f32) * D_MODEL ** -0.5,
        "pool_w_grp": nrm(ks[9], (N_POOL_LAYERS, POOL_GROUPS, POOL_GROUP_DIM, POOL_GROUP_DIM), f32) * POOL_GROUP_DIM ** -0.5,
        "pool_b_grp": nrm(ks[10], (N_POOL_LAYERS, POOL_GROUPS, POOL_GROUP_DIM), f32) * 0.01,
        "pool_scale": 1.0 + 0.1 * nrm(ks[11], (N_POOL_LAYERS, POOL_WIDTH), f32),
        "pool_w_out": nrm(ks[12], (N_POOL_LAYERS, POOL_WIDTH, D_MODEL), f32) * POOL_WIDTH ** -0.5,
        "mla_w_in": nrm(ks[13], (N_MLA_LAYERS, D_MODEL, MLA_IN_WIDTH), f32) * D_MODEL ** -0.5,
        "mla_q_norm": 1.0 + 0.1 * nrm(ks[14], (N_MLA_LAYERS, Q_LORA_RANK), f32),
        "mla_kv_norm": 1.0 + 0.1 * nrm(ks[15], (N_MLA_LAYERS, KV_LORA_RANK), f32),
        "mla_w_uq": nrm(ks[16], (N_MLA_LAYERS, Q_LORA_RANK, MLA_HEADS * (QK_NOPE_DIM + QK_ROPE_DIM)), f32) * Q_LORA_RANK ** -0.5,
        "mla_w_ukv": nrm(ks[17], (N_MLA_LAYERS, KV_LORA_RANK, MLA_HEADS * (QK_NOPE_DIM + V_HEAD_DIM)), f32) * KV_LORA_RANK ** -0.5,
        "mla_w_out": nrm(ks[18], (N_MLA_LAYERS, MLA_WIDTH, D_MODEL), f32) * MLA_WIDTH ** -0.5,
    }


def reference(x, c, ctx, c_ctx, ada_w, ada_b, pre_norm, post_norm,
              pool_w_in, pool_w_grp, pool_b_grp, pool_scale, pool_w_out,
              mla_w_in, mla_q_norm, mla_kv_norm, mla_w_uq, mla_w_ukv, mla_w_out):
    for i in range(DEPTH):
        last = i == DEPTH - 1
        j = i // N_MIXERS
        is_pool = (i % N_MIXERS) == 0
        need_ctx_out = not last
        need_ctx_in = need_ctx_out or not is_pool
        shift, scale, gate = jnp.split(jax.nn.silu(c) @ ada_w[i] + ada_b[i], 3, axis=-1)
        h_lat = rms_norm(x, pre_norm[i]) * (1 + scale[:, None]) + shift[:, None]
        h_ctx = None
        gate_c = None
        if need_ctx_in:
            shift_c, scale_c, gate_c = jnp.split(jax.nn.silu(c_ctx) @ ada_w[i] + ada_b[i], 3, axis=-1)
            h_ctx = rms_norm(ctx, pre_norm[i]) * (1 + scale_c) + shift_c
        if is_pool:
            y_lat = pool_branch(h_lat, pool_w_in[j], pool_w_grp[j], pool_b_grp[j], pool_scale[j], pool_w_out[j])
            y_ctx = (pool_branch(h_ctx, pool_w_in[j], pool_w_grp[j], pool_b_grp[j], pool_scale[j], pool_w_out[j])
                     if need_ctx_out else None)
        else:
            y_lat, y_ctx = mla_branch(h_lat, h_ctx, mla_w_in[j], mla_q_norm[j], mla_kv_norm[j],
                                      mla_w_uq[j], mla_w_ukv[j], mla_w_out[j], need_ctx_out)
        x = x + gate[:, None] * rms_norm(y_lat, post_norm[i])
        if need_ctx_out:
            ctx = ctx + gate_c * rms_norm(y_ctx, post_norm[i])
    return x
```

```python
import functools

import jax
import jax.numpy as jnp
from jax import lax
from jax.experimental import pallas as pl
from jax.experimental.pallas import tpu as pltpu

GRID_W = 64
POOL_GROUPS = 4
POOL_WINDOWS = (2, 4, 8, 16)
MLA_HEADS = 16
Q_LORA_RANK = 512
KV_LORA_RANK = 512
QK_NOPE_DIM = 128
QK_ROPE_DIM = 64
V_HEAD_DIM = 128
MLA_SCALE = (QK_NOPE_DIM + QK_ROPE_DIM) ** -0.5
ROPE_BASE = 10000.0
NORM_EPS = 1e-6

V7X_LANES = 128
V7X_SUBLANES = 8
V7X_VMEM_LIMIT_BYTES = 56 * 1024 * 1024

MOD_ROWS = 16
HALO = V7X_SUBLANES
ROW_TILE = 256
Q_TILE = 512
MOD_COL_TILE = 1024

BF16 = jnp.bfloat16
F32 = jnp.float32


def _silu(v):
    return v * (1.0 / (1.0 + jnp.exp(-v)))


def _rms_scale(v):
    return lax.rsqrt(jnp.mean(v * v, axis=-1, keepdims=True) + NORM_EPS)


def _resident(shape):
    nd = len(shape)
    return pl.BlockSpec(shape, lambda *_: (0,) * nd, pipeline_mode=pl.Buffered(1))


def _params(n_grid):
    return pltpu.CompilerParams(
        dimension_semantics=("arbitrary",) * n_grid,
        vmem_limit_bytes=V7X_VMEM_LIMIT_BYTES)


def _mod_kernel(cc_ref, w_ref, b_ref, o_ref):
    a = _silu(cc_ref[...]).astype(BF16)
    o_ref[0] = jnp.dot(a, w_ref[0].astype(BF16), preferred_element_type=F32) + b_ref[0]


def _modulation(cc, ada_w, ada_b):
    depth, d, n = ada_w.shape
    return pl.pallas_call(
        _mod_kernel,
        out_shape=jax.ShapeDtypeStruct((depth, MOD_ROWS, n), F32),
        grid=(depth, n // MOD_COL_TILE),
        in_specs=[
            pl.BlockSpec((MOD_ROWS, d), lambda i, j: (0, 0)),
            pl.BlockSpec((1, d, MOD_COL_TILE), lambda i, j: (i, 0, j)),
            pl.BlockSpec((1, 1, MOD_COL_TILE), lambda i, j: (i, 0, j)),
        ],
        out_specs=pl.BlockSpec((1, MOD_ROWS, MOD_COL_TILE), lambda i, j: (i, 0, j)),
        compiler_params=_params(2),
        name="adaln_mod",
    )(cc, ada_w, ada_b.reshape(depth, 1, n))


def _pool_kernel(x_ref, xprev_ref, xnext_ref, mod_ref, pre_ref, post_ref, win_ref, wgrp_ref,
                 bgrp_ref, pscale_ref, wout_ref, o_ref, y_scr, *, seq_len):
    t = pl.program_id(1)
    nt = pl.num_programs(1)
    tm = x_ref.shape[1]
    width = wout_ref.shape[0]
    gdim = width // POOL_GROUPS
    n_ext = tm + 2 * HALO

    mod = mod_ref[0]
    shift, gate = mod[0:1], mod[2:3]
    gain = pre_ref[...] * (1.0 + mod[1:2])

    def modulate(v):
        return v * _rms_scale(v) * gain + shift

    x_main = x_ref[0]
    h_main = modulate(x_main).astype(BF16)
    halo = modulate(jnp.concatenate([xnext_ref[0], xprev_ref[0]], axis=0))
    row = lax.broadcasted_iota(jnp.int32, (2 * HALO, 1), 0)
    first_ok = jnp.where(t < nt - 1, 0, HALO)
    end_ok = jnp.where(t > 0, 2 * HALO, HALO)
    halo = jnp.where((row >= first_ok) & (row < end_ok), halo, 0.0).astype(BF16)
    h_ext = jnp.concatenate([h_main, halo], axis=0)

    pos = t * tm + lax.broadcasted_iota(jnp.int32, (tm, 1), 0)
    for g in range(POOL_GROUPS):
        cols = slice(g * gdim, (g + 1) * gdim)
        window = POOL_WINDOWS[g]
        left = window // 2
        right = window - 1 - left
        u = jnp.dot(h_ext, win_ref[:, cols], preferred_element_type=F32)
        acc = u
        span = 1
        while span < window:
            acc = acc + pltpu.roll(acc, span, axis=0)
            span *= 2
        if right:
            acc = pltpu.roll(acc, n_ext - right, axis=0)
        count = jnp.minimum(pos + right + 1, seq_len) - jnp.maximum(pos - left, 0)
        pooled = acc[:tm] * (1.0 / count.astype(F32)) - u[:tm]
        mixed = jnp.dot(pooled.astype(BF16), wgrp_ref[g], preferred_element_type=F32) + bgrp_ref[:, cols]
        gt = jnp.dot(h_main, win_ref[:, width + g * gdim: width + (g + 1) * gdim],
                     preferred_element_type=F32)
        y_scr[:, cols] = (mixed * pscale_ref[:, cols] * _silu(gt)).astype(BF16)

    out = jnp.dot(y_scr[...], wout_ref[...], preferred_element_type=F32)
    o_ref[0] = x_main + gate * (out * _rms_scale(out) * post_ref[...])


def _pool_layer(x, mod, mod_row, pre, post, w_in, w_grp, b_grp, p_scale, w_out):
    b, seq_len, d = x.shape
    tm = ROW_TILE
    nt = seq_len // tm
    width = w_out.shape[0]
    halo_blocks = tm // HALO
    last_halo_block = seq_len // HALO - 1
    kernel = functools.partial(_pool_kernel, seq_len=seq_len)
    return pl.pallas_call(
        kernel,
        out_shape=jax.ShapeDtypeStruct(x.shape, F32),
        grid=(b, nt),
        in_specs=[
            pl.BlockSpec((1, tm, d), lambda i, t: (i, t, 0)),
            pl.BlockSpec((1, HALO, d), lambda i, t: (i, jnp.maximum(t * halo_blocks - 1, 0), 0)),
            pl.BlockSpec((1, HALO, d), lambda i, t: (i, jnp.minimum((t + 1) * halo_blocks, last_halo_block), 0)),
            pl.BlockSpec((1, 3, d), lambda i, t: (mod_row(i), 0, 0)),
            _resident((1, d)),
            _resident((1, d)),
            _resident(w_in.shape),
            _resident(w_grp.shape),
            _resident((1, width)),
            _resident((1, width)),
            _resident(w_out.shape),
        ],
        out_specs=pl.BlockSpec((1, tm, d), lambda i, t: (i, t, 0)),
        scratch_shapes=[pltpu.VMEM((tm, width), BF16)],
        compiler_params=_params(2),
        name="pool_layer",
    )(x, x, x, mod, pre, post, w_in, w_grp, b_grp, p_scale, w_out)


def _mla_proj_kernel(ctx_ref, x_ref, mod_ref, pre_ref, t1_ref, t2_ref, wa_ref, wg_ref, wuq_ref,
                     wukv_ref, qn_ref, kvn_ref, q_ref, sg_ref, kv_ref, kr_ref):
    t = pl.program_id(1)
    mod = mod_ref[0]
    gain = pre_ref[...] * (1.0 + mod[1:2])
    xv = jnp.where(t == 0, ctx_ref[0], x_ref[0])
    hb = (xv * _rms_scale(xv) * gain + mod[0:1]).astype(BF16)

    pa = jnp.dot(hb, wa_ref[...], preferred_element_type=F32)
    c_q = pa[:, :Q_LORA_RANK]
    c_kv = pa[:, Q_LORA_RANK:Q_LORA_RANK + KV_LORA_RANK]
    kr = pa[:, Q_LORA_RANK + KV_LORA_RANK:]

    ckv_n = (c_kv * _rms_scale(c_kv) * kvn_ref[...]).astype(BF16)
    kv_ref[0] = jnp.dot(ckv_n, wukv_ref[...], preferred_element_type=F32).astype(BF16)
    t1 = t1_ref[...]
    kr_ref[0] = (kr * t1 + pltpu.roll(kr, QK_ROPE_DIM, axis=1) * t2_ref[...]).astype(BF16)

    @pl.when(t > 0)
    def _():
        cq_n = (c_q * _rms_scale(c_q) * (qn_ref[...] * MLA_SCALE)).astype(BF16)
        z = jnp.dot(cq_n, wuq_ref[...], preferred_element_type=F32)
        hd = 2 * V7X_LANES
        for h in range(MLA_HEADS):
            q_ref[0, :, h * hd:h * hd + V7X_LANES] = z[:, h * hd:h * hd + V7X_LANES].astype(BF16)
            q_ref[0, :, h * hd + V7X_LANES:(h + 1) * hd] = (z[:, h * hd + V7X_LANES:(h + 1) * hd] * t1).astype(BF16)
        sg_ref[0] = _silu(jnp.dot(hb, wg_ref[...], preferred_element_type=F32)).astype(BF16)


def _mla_project(x, ctx, mod, pre, t1, t2, w_a, w_g, w_uq, w_ukv, q_norm, kv_norm):
    b, seq_len, d = x.shape
    tm = ROW_TILE
    assert ctx.shape[1] == tm
    nt = seq_len // tm + 1
    lk = seq_len + tm
    ctx_row = b
    lat = lambda i, t: (i, jnp.maximum(t - 1, 0), 0)
    return pl.pallas_call(
        _mla_proj_kernel,
        out_shape=(
            jax.ShapeDtypeStruct((b, seq_len, w_uq.shape[1]), BF16),
            jax.ShapeDtypeStruct((b, seq_len, w_g.shape[1]), BF16),
            jax.ShapeDtypeStruct((b, lk, w_ukv.shape[1]), BF16),
            jax.ShapeDtypeStruct((b, lk, V7X_LANES), BF16),
        ),
        grid=(b, nt),
        in_specs=[
            pl.BlockSpec((1, tm, d), lambda i, t: (i, 0, 0)),
            pl.BlockSpec((1, tm, d), lat),
            pl.BlockSpec((1, 3, d), lambda i, t: (jnp.where(t == 0, ctx_row, i), 0, 0)),
            _resident((1, d)),
            pl.BlockSpec((tm, V7X_LANES), lambda i, t: (t, 0)),
            pl.BlockSpec((tm, V7X_LANES), lambda i, t: (t, 0)),
            _resident(w_a.shape),
            _resident(w_g.shape),
            _resident(w_uq.shape),
            _resident(w_ukv.shape),
            _resident((1, Q_LORA_RANK)),
            _resident((1, KV_LORA_RANK)),
        ],
        out_specs=(
            pl.BlockSpec((1, tm, w_uq.shape[1]), lat),
            pl.BlockSpec((1, tm, w_g.shape[1]), lat),
            pl.BlockSpec((1, tm, w_ukv.shape[1]), lambda i, t: (i, t, 0)),
            pl.BlockSpec((1, tm, V7X_LANES), lambda i, t: (i, t, 0)),
        ),
        compiler_params=_params(2),
        name="mla_project",
    )(ctx, x, mod, pre, t1, t2, w_a, w_g, w_uq, w_ukv, q_norm, kv_norm)


def _attn_kernel(q_ref, kn_ref, kr_ref, v_ref, o_ref, kcat_scr):
    @pl.when(pl.program_id(2) == 0)
    def _():
        kcat_scr[:, :V7X_LANES] = kn_ref[0]
        kcat_scr[:, V7X_LANES:] = kr_ref[0]

    s = lax.dot_general(q_ref[0], kcat_scr[...], (((1,), (1,)), ((), ())),
                        preferred_element_type=F32)
    p = jnp.exp(s - jnp.max(s, axis=-1, keepdims=True))
    denom = jnp.sum(p, axis=-1, keepdims=True)
    o = jnp.dot(p.astype(BF16), v_ref[0], preferred_element_type=F32)
    o_ref[0] = (o * (1.0 / denom)).astype(BF16)


def _attention(q, kv, kr):
    b, seq_len, _ = q.shape
    lk = kv.shape[1]
    tq = Q_TILE
    return pl.pallas_call(
        _attn_kernel,
        out_shape=jax.ShapeDtypeStruct((b, seq_len, MLA_HEADS * V_HEAD_DIM), BF16),
        grid=(b, MLA_HEADS, seq_len // tq),
        in_specs=[
            pl.BlockSpec((1, tq, 2 * V7X_LANES), lambda i, h, j: (i, j, h)),
            pl.BlockSpec((1, lk, QK_NOPE_DIM), lambda i, h, j: (i, 0, h)),
            pl.BlockSpec((1, lk, V7X_LANES), lambda i, h, j: (i, 0, 0)),
            pl.BlockSpec((1, lk, V_HEAD_DIM), lambda i, h, j: (i, 0, MLA_HEADS + h)),
        ],
        out_specs=pl.BlockSpec((1, tq, V_HEAD_DIM), lambda i, h, j: (i, j, h)),
        scratch_shapes=[pltpu.VMEM((lk, 2 * V7X_LANES), BF16)],
        compiler_params=_params(3),
        name="mla_attention",
    )(q, kv, kr, kv)


def _mla_out_kernel(o_ref, sg_ref, x_ref, mod_ref, post_ref, wout_ref, out_ref):
    y = (o_ref[0].astype(F32) * sg_ref[0].astype(F32)).astype(BF16)
    out = jnp.dot(y, wout_ref[...], preferred_element_type=F32)
    out_ref[0] = x_ref[0] + mod_ref[0][2:3] * (out * _rms_scale(out) * post_ref[...])


def _mla_output(o, sg, x, mod, post, w_out):
    b, seq_len, d = x.shape
    tm = ROW_TILE
    width = o.shape[2]
    return pl.pallas_call(
        _mla_out_kernel,
        out_shape=jax.ShapeDtypeStruct(x.shape, F32),
        grid=(b, seq_len // tm),
        in_specs=[
            pl.BlockSpec((1, tm, width), lambda i, t: (i, t, 0)),
            pl.BlockSpec((1, tm, width), lambda i, t: (i, t, 0)),
            pl.BlockSpec((1, tm, d), lambda i, t: (i, t, 0)),
            pl.BlockSpec((1, 3, d), lambda i, t: (i, 0, 0)),
            _resident((1, d)),
            _resident(w_out.shape),
        ],
        out_specs=pl.BlockSpec((1, tm, d), lambda i, t: (i, t, 0)),
        compiler_params=_params(2),
        name="mla_output",
    )(o, sg, x, mod, post, w_out)


def _swap_rope_halves(w):
    n = QK_ROPE_DIM // 4
    return w.reshape(w.shape[:-1] + (2, 2, n))[..., ::-1, :].reshape(w.shape)


def _rope_tables(seq_len, ctx_len):
    n = QK_ROPE_DIM // 4
    idx = jnp.arange(seq_len, dtype=jnp.int32)
    row = (idx // GRID_W).astype(F32)
    col = (idx % GRID_W).astype(F32)
    freqs = ROPE_BASE ** (-jnp.arange(n, dtype=F32) / n)
    ar, ac = row[:, None] * freqs, col[:, None] * freqs
    cos = jnp.concatenate([jnp.cos(ar), jnp.cos(ar), jnp.cos(ac), jnp.cos(ac)], axis=-1)
    sin = jnp.concatenate([-jnp.sin(ar), jnp.sin(ar), -jnp.sin(ac), jnp.sin(ac)], axis=-1)
    one = jnp.ones((ctx_len, QK_ROPE_DIM), F32)
    zero = jnp.zeros((ctx_len, QK_ROPE_DIM), F32)
    t1 = jnp.concatenate([jnp.concatenate([one, zero], -1), jnp.concatenate([cos, sin], -1)], 0)
    t2 = jnp.concatenate([jnp.concatenate([zero, one], -1), jnp.concatenate([sin, cos], -1)], 0)
    return t1, t2


def kernel(x, c, ctx, c_ctx, ada_w, ada_b, pre_norm, post_norm, pool_w_in, pool_w_grp, pool_b_grp,
           pool_scale, pool_w_out, mla_w_in, mla_q_norm, mla_kv_norm, mla_w_uq, mla_w_ukv, mla_w_out):
    b, seq_len, d = x.shape
    ctx_len = ctx.shape[1]
    assert b + 1 <= MOD_ROWS

    cc = jnp.concatenate([c, c_ctx[None, :], jnp.zeros((MOD_ROWS - b - 1, d), F32)], axis=0)
    mod = _modulation(cc, ada_w, ada_b).reshape(ada_w.shape[0], MOD_ROWS, 3, d)

    width = pool_w_out.shape[1]
    pool_args = (pre_norm[0].reshape(1, d), post_norm[0].reshape(1, d), pool_w_in[0].astype(BF16),
                 pool_w_grp[0].astype(BF16), pool_b_grp[0].reshape(1, width),
                 pool_scale[0].reshape(1, width), pool_w_out[0].astype(BF16))
    x1 = _pool_layer(x, mod[0], lambda i: i, *pool_args)
    ctx1 = _pool_layer(ctx, mod[0], lambda i: b, *pool_args)

    w_in = mla_w_in[0]
    r0 = Q_LORA_RANK + KV_LORA_RANK
    w_kr = w_in[:, r0:r0 + QK_ROPE_DIM]
    w_a = jnp.concatenate([w_in[:, :r0], w_kr, _swap_rope_halves(w_kr)], axis=1).astype(BF16)
    w_g = w_in[:, r0 + QK_ROPE_DIM:].astype(BF16)
    w_uq = mla_w_uq[0].reshape(Q_LORA_RANK, MLA_HEADS, QK_NOPE_DIM + QK_ROPE_DIM)
    w_uq_rope = w_uq[..., QK_NOPE_DIM:]
    w_uq = jnp.concatenate([w_uq[..., :QK_NOPE_DIM], w_uq_rope, _swap_rope_halves(w_uq_rope)], axis=-1)
    w_uq = w_uq.reshape(Q_LORA_RANK, -1).astype(BF16)
    w_ukv = mla_w_ukv[0].reshape(KV_LORA_RANK, MLA_HEADS, QK_NOPE_DIM + V_HEAD_DIM)
    w_ukv = jnp.concatenate([w_ukv[..., :QK_NOPE_DIM].reshape(KV_LORA_RANK, -1),
                             w_ukv[..., QK_NOPE_DIM:].reshape(KV_LORA_RANK, -1)], axis=1).astype(BF16)
    t1, t2 = _rope_tables(seq_len, ctx_len)

    q, sg, kv, kr = _mla_project(x1, ctx1, mod[1], pre_norm[1].reshape(1, d), t1, t2, w_a, w_g, w_uq, w_ukv,
                                 mla_q_norm[0].reshape(1, -1), mla_kv_norm[0].reshape(1, -1))
    o = _attention(q, kv, kr)
    return _mla_output(o, sg, x1, mod[1], post_norm[1].reshape(1, d), mla_w_out[0].astype(BF16))
```

```python
import functools

import jax
import jax.numpy as jnp
from jax import lax
from jax.experimental import pallas as pl
from jax.experimental.pallas import tpu as pltpu

GRID_W = 64
POOL_GROUPS = 4
POOL_WINDOWS = (2, 4, 8, 16)
MLA_HEADS = 16
Q_LORA_RANK = 512
KV_LORA_RANK = 512
QK_NOPE_DIM = 128
QK_ROPE_DIM = 64
V_HEAD_DIM = 128
MLA_SCALE = (QK_NOPE_DIM + QK_ROPE_DIM) ** -0.5
ROPE_BASE = 10000.0
NORM_EPS = 1e-6

V7X_LANES = 128
V7X_SUBLANES = 8
V7X_VMEM_LIMIT_BYTES = 56 * 1024 * 1024

MOD_ROWS = 16
HALO = V7X_SUBLANES
ROW_TILE = 256
Q_SUBTILE = 512
KEY_CHUNKS = 3
MOD_COL_TILE = 1024
LOG2_E = 1.4426950408889634

BF16 = jnp.bfloat16
F32 = jnp.float32


def _silu(v):
    return v * (1.0 / (1.0 + jnp.exp(-v)))


def _rms_scale(v):
    return lax.rsqrt(jnp.mean(v * v, axis=-1, keepdims=True) + NORM_EPS)


def _resident(shape):
    nd = len(shape)
    return pl.BlockSpec(shape, lambda *_: (0,) * nd, pipeline_mode=pl.Buffered(1))


def _params(n_grid):
    return pltpu.CompilerParams(
        dimension_semantics=("arbitrary",) * n_grid,
        vmem_limit_bytes=V7X_VMEM_LIMIT_BYTES)


def _mod_kernel(cc_ref, w_ref, b_ref, o_ref):
    a = _silu(cc_ref[...]).astype(BF16)
    o_ref[0] = jnp.dot(a, w_ref[0].astype(BF16), preferred_element_type=F32) + b_ref[0]


def _modulation(cc, ada_w, ada_b):
    depth, d, n = ada_w.shape
    return pl.pallas_call(
        _mod_kernel,
        out_shape=jax.ShapeDtypeStruct((depth, MOD_ROWS, n), F32),
        grid=(depth, n // MOD_COL_TILE),
        in_specs=[
            pl.BlockSpec((MOD_ROWS, d), lambda i, j: (0, 0)),
            pl.BlockSpec((1, d, MOD_COL_TILE), lambda i, j: (i, 0, j)),
            pl.BlockSpec((1, 1, MOD_COL_TILE), lambda i, j: (i, 0, j)),
        ],
        out_specs=pl.BlockSpec((1, MOD_ROWS, MOD_COL_TILE), lambda i, j: (i, 0, j)),
        compiler_params=_params(2),
        name="adaln_mod",
    )(cc, ada_w, ada_b.reshape(depth, 1, n))


def _pool_kernel(x_ref, xprev_ref, xnext_ref, mod_ref, pre_ref, post_ref, win_ref, wgrp_ref,
                 bgrp_ref, pscale_ref, wout_ref, o_ref, y_scr, *, seq_len):
    t = pl.program_id(1)
    nt = pl.num_programs(1)
    tm = x_ref.shape[1]
    width = wout_ref.shape[0]
    gdim = width // POOL_GROUPS
    n_ext = tm + 2 * HALO

    mod = mod_ref[0]
    shift, gate = mod[0:1], mod[2:3]
    gain = pre_ref[...] * (1.0 + mod[1:2])

    def modulate(v):
        return v * _rms_scale(v) * gain + shift

    x_main = x_ref[0]
    h_main = modulate(x_main).astype(BF16)
    halo = modulate(jnp.concatenate([xnext_ref[0], xprev_ref[0]], axis=0))
    row = lax.broadcasted_iota(jnp.int32, (2 * HALO, 1), 0)
    first_ok = jnp.where(t < nt - 1, 0, HALO)
    end_ok = jnp.where(t > 0, 2 * HALO, HALO)
    halo = jnp.where((row >= first_ok) & (row < end_ok), halo, 0.0).astype(BF16)
    h_ext = jnp.concatenate([h_main, halo], axis=0)

    pos = t * tm + lax.broadcasted_iota(jnp.int32, (tm, 1), 0)
    for g in range(POOL_GROUPS):
        cols = slice(g * gdim, (g + 1) * gdim)
        window = POOL_WINDOWS[g]
        left = window // 2
        right = window - 1 - left
        u = jnp.dot(h_ext, win_ref[:, cols], preferred_element_type=F32)
        acc = u
        span = 1
        while span < window:
            acc = acc + pltpu.roll(acc, span, axis=0)
            span *= 2
        if right:
            acc = pltpu.roll(acc, n_ext - right, axis=0)
        count = jnp.minimum(pos + right + 1, seq_len) - jnp.maximum(pos - left, 0)
        pooled = acc[:tm] * (1.0 / count.astype(F32)) - u[:tm]
        mixed = jnp.dot(pooled.astype(BF16), wgrp_ref[g], preferred_element_type=F32) + bgrp_ref[:, cols]
        gt = jnp.dot(h_main, win_ref[:, width + g * gdim: width + (g + 1) * gdim],
                     preferred_element_type=F32)
        y_scr[:, cols] = (mixed * pscale_ref[:, cols] * _silu(gt)).astype(BF16)

    out = jnp.dot(y_scr[...], wout_ref[...], preferred_element_type=F32)
    o_ref[0] = x_main + gate * (out * _rms_scale(out) * post_ref[...])


def _pool_layer(x, mod, mod_row, pre, post, w_in, w_grp, b_grp, p_scale, w_out):
    b, seq_len, d = x.shape
    tm = ROW_TILE
    nt = seq_len // tm
    width = w_out.shape[0]
    halo_blocks = tm // HALO
    last_halo_block = seq_len // HALO - 1
    kernel = functools.partial(_pool_kernel, seq_len=seq_len)
    return pl.pallas_call(
        kernel,
        out_shape=jax.ShapeDtypeStruct(x.shape, F32),
        grid=(b, nt),
        in_specs=[
            pl.BlockSpec((1, tm, d), lambda i, t: (i, t, 0)),
            pl.BlockSpec((1, HALO, d), lambda i, t: (i, jnp.maximum(t * halo_blocks - 1, 0), 0)),
            pl.BlockSpec((1, HALO, d), lambda i, t: (i, jnp.minimum((t + 1) * halo_blocks, last_halo_block), 0)),
            pl.BlockSpec((1, 3, d), lambda i, t: (mod_row(i), 0, 0)),
            _resident((1, d)),
            _resident((1, d)),
            _resident(w_in.shape),
            _resident(w_grp.shape),
            _resident((1, width)),
            _resident((1, width)),
            _resident(w_out.shape),
        ],
        out_specs=pl.BlockSpec((1, tm, d), lambda i, t: (i, t, 0)),
        scratch_shapes=[pltpu.VMEM((tm, width), BF16)],
        compiler_params=_params(2),
        name="pool_layer",
    )(x, x, x, mod, pre, post, w_in, w_grp, b_grp, p_scale, w_out)


def _mla_proj_kernel(ctx_ref, x_ref, mod_ref, pre_ref, t1_ref, t2_ref, wa_ref, wg_ref, wuq_ref,
                     wuk_ref, wuvt_ref, qn_ref, kvn_ref, q_ref, sg_ref, kn_ref, vt_ref, kr_ref):
    t = pl.program_id(1)
    mod = mod_ref[0]
    gain = pre_ref[...] * (1.0 + mod[1:2])
    xv = jnp.where(t == 0, ctx_ref[0], x_ref[0])
    hb = (xv * _rms_scale(xv) * gain + mod[0:1]).astype(BF16)

    pa = jnp.dot(hb, wa_ref[...], preferred_element_type=F32)
    c_q = pa[:, :Q_LORA_RANK]
    c_kv = pa[:, Q_LORA_RANK:Q_LORA_RANK + KV_LORA_RANK]
    kr = pa[:, Q_LORA_RANK + KV_LORA_RANK:]

    ckv_n = (c_kv * _rms_scale(c_kv) * kvn_ref[...]).astype(BF16)
    kn_ref[0] = jnp.dot(ckv_n, wuk_ref[...], preferred_element_type=F32).astype(BF16)
    vt_ref[0] = lax.dot_general(wuvt_ref[...], ckv_n, (((1,), (1,)), ((), ())),
                                preferred_element_type=F32).astype(BF16)
    t1 = t1_ref[...]
    kr_ref[0] = (kr * t1 + pltpu.roll(kr, QK_ROPE_DIM, axis=1) * t2_ref[...]).astype(BF16)

    @pl.when(t > 0)
    def _():
        cq_n = (c_q * _rms_scale(c_q) * (qn_ref[...] * (MLA_SCALE * LOG2_E))).astype(BF16)
        z = jnp.dot(cq_n, wuq_ref[...], preferred_element_type=F32)
        hd = 2 * V7X_LANES
        for h in range(MLA_HEADS):
            q_ref[0, :, h * hd:h * hd + V7X_LANES] = z[:, h * hd:h * hd + V7X_LANES].astype(BF16)
            q_ref[0, :, h * hd + V7X_LANES:(h + 1) * hd] = (z[:, h * hd + V7X_LANES:(h + 1) * hd] * t1).astype(BF16)
        sg_ref[0] = _silu(jnp.dot(hb, wg_ref[...], preferred_element_type=F32)).astype(BF16)


def _mla_project(x, ctx, mod, pre, t1, t2, w_a, w_g, w_uq, w_uk, w_uvt, q_norm, kv_norm):
    b, seq_len, d = x.shape
    tm = ROW_TILE
    assert ctx.shape[1] == tm
    nt = seq_len // tm + 1
    lk = seq_len + tm
    ctx_row = b
    lat = lambda i, t: (i, jnp.maximum(t - 1, 0), 0)
    return pl.pallas_call(
        _mla_proj_kernel,
        out_shape=(
            jax.ShapeDtypeStruct((b, seq_len, w_uq.shape[1]), BF16),
            jax.ShapeDtypeStruct((b, seq_len, w_g.shape[1]), BF16),
            jax.ShapeDtypeStruct((b, lk, w_uk.shape[1]), BF16),
            jax.ShapeDtypeStruct((b, w_uvt.shape[0], lk), BF16),
            jax.ShapeDtypeStruct((b, lk, V7X_LANES), BF16),
        ),
        grid=(b, nt),
        in_specs=[
            pl.BlockSpec((1, tm, d), lambda i, t: (i, 0, 0)),
            pl.BlockSpec((1, tm, d), lat),
            pl.BlockSpec((1, 3, d), lambda i, t: (jnp.where(t == 0, ctx_row, i), 0, 0)),
            _resident((1, d)),
            pl.BlockSpec((tm, V7X_LANES), lambda i, t: (t, 0)),
            pl.BlockSpec((tm, V7X_LANES), lambda i, t: (t, 0)),
            _resident(w_a.shape),
            _resident(w_g.shape),
            _resident(w_uq.shape),
            _resident(w_uk.shape),
            _resident(w_uvt.shape),
            _resident((1, Q_LORA_RANK)),
            _resident((1, KV_LORA_RANK)),
        ],
        out_specs=(
            pl.BlockSpec((1, tm, w_uq.shape[1]), lat),
            pl.BlockSpec((1, tm, w_g.shape[1]), lat),
            pl.BlockSpec((1, tm, w_uk.shape[1]), lambda i, t: (i, t, 0)),
            pl.BlockSpec((1, w_uvt.shape[0], tm), lambda i, t: (i, 0, t)),
            pl.BlockSpec((1, tm, V7X_LANES), lambda i, t: (i, t, 0)),
        ),
        compiler_params=_params(2),
        name="mla_project",
    )(ctx, x, mod, pre, t1, t2, w_a, w_g, w_uq, w_uk, w_uvt, q_norm, kv_norm)


def _attn_kernel(q_ref, kn_ref, kr_ref, vt_ref, o_ref, kcat_scr):
    kcat_scr[:, :V7X_LANES] = kn_ref[0]
    kcat_scr[:, V7X_LANES:] = kr_ref[0]

    ck =kcat_scr.shape[0] // KEY_CHUNKS
    tq = Q_SUBTILE
    n_sub = q_ref.shape[1] // tq

    def scores(j):
        q = q_ref[0, j * tq:(j + 1) * tq, :]
        return [lax.dot_general(kcat_scr[c * ck:(c + 1) * ck, :], q, (((1,), (1,)), ((), ())),
                                preferred_element_type=F32)
                for c in range(KEY_CHUNKS)]

    st_next = scores(0)
    for j in range(n_sub):
        st_cur = st_next
        if j + 1 < n_sub:
            st_next = scores(j + 1)
        parts = []
        for c in range(KEY_CHUNKS):
            st = st_cur[c]
            m_c = jnp.max(st, axis=0, keepdims=True)
            p = jnp.exp2(st - m_c)
            l_c = jnp.sum(p, axis=0, keepdims=True)
            o_c = jnp.dot(vt_ref[0, :, c * ck:(c + 1) * ck], p.astype(BF16),
                          preferred_element_type=F32)
            parts.append((m_c, l_c, o_c))
        m = parts[0][0]
        for m_c, _, _ in parts[1:]:
            m = jnp.maximum(m, m_c)
        acc = den = None
        for m_c, l_c, o_c in parts:
            w = jnp.exp2(m_c - m)
            acc = o_c * w if acc is None else acc + o_c * w
            den = l_c * w if den is None else den + l_c * w
        o_ref[0, j * tq:(j + 1) * tq, :] = (acc * (1.0 / den)).T.astype(BF16)


def _attention(q, kn, kr, vt):
    b, seq_len, _ = q.shape
    lk = kn.shape[1]
    assert lk % (KEY_CHUNKS * 2 * V7X_LANES) == 0
    assert seq_len % Q_SUBTILE == 0
    return pl.pallas_call(
        _attn_kernel,
        out_shape=jax.ShapeDtypeStruct((b, seq_len, MLA_HEADS * V_HEAD_DIM), BF16),
        grid=(b, MLA_HEADS),
        in_specs=[
            pl.BlockSpec((1, seq_len, 2 * V7X_LANES), lambda i, h: (i, 0, h)),
            pl.BlockSpec((1, lk, QK_NOPE_DIM), lambda i, h: (i, 0, h)),
            pl.BlockSpec((1, lk, V7X_LANES), lambda i, h: (i, 0, 0)),
            pl.BlockSpec((1, V_HEAD_DIM, lk), lambda i, h: (i, h, 0)),
        ],
        out_specs=pl.BlockSpec((1, seq_len, V_HEAD_DIM), lambda i, h: (i, 0, h)),
        scratch_shapes=[pltpu.VMEM((lk, 2 * V7X_LANES), BF16)],
        compiler_params=_params(2),
        name="mla_attention",
    )(q, kn, kr, vt)


def _mla_out_kernel(o_ref, sg_ref, x_ref, mod_ref, post_ref, wout_ref, out_ref):
    y = (o_ref[0].astype(F32) * sg_ref[0].astype(F32)).astype(BF16)
    out = jnp.dot(y, wout_ref[...], preferred_element_type=F32)
    out_ref[0] = x_ref[0] + mod_ref[0][2:3] * (out * _rms_scale(out) * post_ref[...])


def _mla_output(o, sg, x, mod, post, w_out):
    b, seq_len, d = x.shape
    tm = ROW_TILE
    width = o.shape[2]
    return pl.pallas_call(
        _mla_out_kernel,
        out_shape=jax.ShapeDtypeStruct(x.shape, F32),
        grid=(b, seq_len // tm),
        in_specs=[
            pl.BlockSpec((1, tm, width), lambda i, t: (i, t, 0)),
            pl.BlockSpec((1, tm, width), lambda i, t: (i, t, 0)),
            pl.BlockSpec((1, tm, d), lambda i, t: (i, t, 0)),
            pl.BlockSpec((1, 3, d), lambda i, t: (i, 0, 0)),
            _resident((1, d)),
            _resident(w_out.shape),
        ],
        out_specs=pl.BlockSpec((1, tm, d), lambda i, t: (i, t, 0)),
        compiler_params=_params(2),
        name="mla_output",
    )(o, sg, x, mod, post, w_out)


def _swap_rope_halves(w):
    n = QK_ROPE_DIM // 4
    return w.reshape(w.shape[:-1] + (2, 2, n))[..., ::-1, :].reshape(w.shape)


def _rope_tables(seq_len, ctx_len):
    n = QK_ROPE_DIM // 4
    idx = jnp.arange(seq_len, dtype=jnp.int32)
    row = (idx // GRID_W).astype(F32)
    col = (idx % GRID_W).astype(F32)
    freqs = ROPE_BASE ** (-jnp.arange(n, dtype=F32) / n)
    ar, ac = row[:, None] * freqs, col[:, None] * freqs
    cos = jnp.concatenate([jnp.cos(ar), jnp.cos(ar), jnp.cos(ac), jnp.cos(ac)], axis=-1)
    sin = jnp.concatenate([-jnp.sin(ar), jnp.sin(ar), -jnp.sin(ac), jnp.sin(ac)], axis=-1)
    one = jnp.ones((ctx_len, QK_ROPE_DIM), F32)
    zero = jnp.zeros((ctx_len, QK_ROPE_DIM), F32)
    t1 = jnp.concatenate([jnp.concatenate([one, zero], -1), jnp.concatenate([cos, sin], -1)], 0)
    t2 = jnp.concatenate([jnp.concatenate([zero, one], -1), jnp.concatenate([sin, cos], -1)], 0)
    return t1, t2


def kernel(x, c, ctx, c_ctx, ada_w, ada_b, pre_norm, post_norm, pool_w_in, pool_w_grp, pool_b_grp,
           pool_scale, pool_w_out, mla_w_in, mla_q_norm, mla_kv_norm, mla_w_uq, mla_w_ukv, mla_w_out):
    b, seq_len, d = x.shape
    ctx_len = ctx.shape[1]
    assert b + 1 <= MOD_ROWS

    cc = jnp.concatenate([c, c_ctx[None, :], jnp.zeros((MOD_ROWS - b - 1, d), F32)], axis=0)
    mod = _modulation(cc, ada_w, ada_b).reshape(ada_w.shape[0], MOD_ROWS, 3, d)

    width = pool_w_out.shape[1]
    pool_args = (pre_norm[0].reshape(1, d), post_norm[0].reshape(1, d), pool_w_in[0].astype(BF16),
                 pool_w_grp[0].astype(BF16), pool_b_grp[0].reshape(1, width),
                 pool_scale[0].reshape(1, width), pool_w_out[0].astype(BF16))
    x1 = _pool_layer(x, mod[0], lambda i: i, *pool_args)
    ctx1 = _pool_layer(ctx, mod[0], lambda i: b, *pool_args)

    w_in = mla_w_in[0]
    r0 = Q_LORA_RANK + KV_LORA_RANK
    w_kr = w_in[:, r0:r0 + QK_ROPE_DIM]
    w_a = jnp.concatenate([w_in[:, :r0], w_kr, _swap_rope_halves(w_kr)], axis=1).astype(BF16)
    w_g = w_in[:, r0 + QK_ROPE_DIM:].astype(BF16)
    w_uq = mla_w_uq[0].reshape(Q_LORA_RANK, MLA_HEADS, QK_NOPE_DIM + QK_ROPE_DIM)
    w_uq_rope = w_uq[..., QK_NOPE_DIM:]
    w_uq = jnp.concatenate([w_uq[..., :QK_NOPE_DIM], w_uq_rope, _swap_rope_halves(w_uq_rope)], axis=-1)
    w_uq = w_uq.reshape(Q_LORA_RANK, -1).astype(BF16)
    w_ukv = mla_w_ukv[0].reshape(KV_LORA_RANK, MLA_HEADS, QK_NOPE_DIM + V_HEAD_DIM)
    w_uk = w_ukv[..., :QK_NOPE_DIM].reshape(KV_LORA_RANK, -1).astype(BF16)
    w_uvt = w_ukv[..., QK_NOPE_DIM:].reshape(KV_LORA_RANK, -1).T.astype(BF16)
    t1, t2 = _rope_tables(seq_len, ctx_len)

    q, sg, kn, vt, kr = _mla_project(x1, ctx1, mod[1], pre_norm[1].reshape(1, d), t1, t2, w_a, w_g, w_uq,
                                     w_uk, w_uvt, mla_q_norm[0].reshape(1, -1), mla_kv_norm[0].reshape(1, -1))
    o = _attention(q, kn, kr, vt)
    return _mla_output(o, sg, x1, mod[1], post_norm[1].reshape(1, d), mla_w_out[0].astype(BF16))
```

```python
import functools

import jax
import jax.numpy as jnp
from jax import lax
from jax.experimental import pallas as pl
from jax.experimental.pallas import tpu as pltpu

GRID_W = 64
POOL_GROUPS = 4
POOL_WINDOWS = (2, 4, 8, 16)
MLA_HEADS = 16
Q_LORA_RANK = 512
KV_LORA_RANK = 512
QK_NOPE_DIM = 128
QK_ROPE_DIM = 64
V_HEAD_DIM = 128
MLA_SCALE = (QK_NOPE_DIM + QK_ROPE_DIM) ** -0.5
ROPE_BASE = 10000.0
NORM_EPS = 1e-6

V7X_LANES = 128
V7X_SUBLANES = 8
V7X_VMEM_LIMIT_BYTES = 56 * 1024 * 1024

MOD_ROWS = 16
HALO = V7X_SUBLANES
ROW_TILE = 256
Q_SUBTILE = 512
KEY_BLOCK = 768
MOD_COL_TILE = 1024
LOG2_E = 1.4426950408889634

BF16 = jnp.bfloat16
F32 = jnp.float32


def _silu(v):
    return v * (1.0 / (1.0 + jnp.exp(-v)))


def _rms_scale(v):
    return lax.rsqrt(jnp.mean(v * v, axis=-1, keepdims=True) + NORM_EPS)


def _resident(shape):
    nd = len(shape)
    return pl.BlockSpec(shape, lambda *_: (0,) * nd, pipeline_mode=pl.Buffered(1))


def _params(n_grid):
    return pltpu.CompilerParams(
        dimension_semantics=("arbitrary",) * n_grid,
        vmem_limit_bytes=V7X_VMEM_LIMIT_BYTES)


def _mod_kernel(cc_ref, w_ref, b_ref, o_ref):
    a = _silu(cc_ref[...]).astype(BF16)
    o_ref[0] = jnp.dot(a, w_ref[0].astype(BF16), preferred_element_type=F32) + b_ref[0]


def _modulation(cc, ada_w, ada_b):
    depth, d, n = ada_w.shape
    return pl.pallas_call(
        _mod_kernel,
        out_shape=jax.ShapeDtypeStruct((depth, MOD_ROWS, n), F32),
        grid=(depth, n // MOD_COL_TILE),
        in_specs=[
            pl.BlockSpec((MOD_ROWS, d), lambda i, j: (0, 0)),
            pl.BlockSpec((1, d, MOD_COL_TILE), lambda i, j: (i, 0, j)),
            pl.BlockSpec((1, 1, MOD_COL_TILE), lambda i, j: (i, 0, j)),
        ],
        out_specs=pl.BlockSpec((1, MOD_ROWS, MOD_COL_TILE), lambda i, j: (i, 0, j)),
        compiler_params=_params(2),
        name="adaln_mod",
    )(cc, ada_w, ada_b.reshape(depth, 1, n))


def _pool_kernel(x_ref, xprev_ref, xnext_ref, mod_ref, pre_ref, post_ref, win_ref, wgrp_ref,
                 bgrp_ref, pscale_ref, wout_ref, o_ref, y_scr, *, seq_len):
    t = pl.program_id(1)
    nt = pl.num_programs(1)
    tm = x_ref.shape[1]
    width = wout_ref.shape[0]
    gdim = width // POOL_GROUPS
    n_ext = tm + 2 * HALO

    mod = mod_ref[0]
    shift, gate = mod[0:1], mod[2:3]
    gain = pre_ref[...] * (1.0 + mod[1:2])

    def modulate(v):
        return v * _rms_scale(v) * gain + shift

    x_main = x_ref[0]
    h_main = modulate(x_main).astype(BF16)
    halo = modulate(jnp.concatenate([xnext_ref[0], xprev_ref[0]], axis=0))
    row = lax.broadcasted_iota(jnp.int32, (2 * HALO, 1), 0)
    first_ok = jnp.where(t < nt - 1, 0, HALO)
    end_ok = jnp.where(t > 0, 2 * HALO, HALO)
    halo = jnp.where((row >= first_ok) & (row < end_ok), halo, 0.0).astype(BF16)
    h_ext = jnp.concatenate([h_main, halo], axis=0)

    pos = t * tm + lax.broadcasted_iota(jnp.int32, (tm, 1), 0)
    for g in range(POOL_GROUPS):
        cols = slice(g * gdim, (g + 1) * gdim)
        window = POOL_WINDOWS[g]
        left = window // 2
        right = window - 1 - left
        u = jnp.dot(h_ext, win_ref[:, cols], preferred_element_type=F32)
        acc = u
        span = 1
        while span < window:
            acc = acc + pltpu.roll(acc, span, axis=0)
            span *= 2
        if right:
            acc = pltpu.roll(acc, n_ext - right, axis=0)
        count = jnp.minimum(pos + right + 1, seq_len) - jnp.maximum(pos - left, 0)
        pooled = acc[:tm] * (1.0 / count.astype(F32)) - u[:tm]
        mixed = jnp.dot(pooled.astype(BF16), wgrp_ref[g], preferred_element_type=F32) + bgrp_ref[:, cols]
        gt = jnp.dot(h_main, win_ref[:, width + g * gdim: width + (g + 1) * gdim],
                     preferred_element_type=F32)
        y_scr[:, cols] = (mixed * pscale_ref[:, cols] * _silu(gt)).astype(BF16)

    out = jnp.dot(y_scr[...], wout_ref[...], preferred_element_type=F32)
    o_ref[0] = x_main + gate * (out * _rms_scale(out) * post_ref[...])


def _pool_layer(x, mod, mod_row, pre, post, w_in, w_grp, b_grp, p_scale, w_out):
    b, seq_len, d = x.shape
    tm = ROW_TILE
    nt = seq_len // tm
    width = w_out.shape[0]
    halo_blocks = tm // HALO
    last_halo_block = seq_len // HALO - 1
    kernel = functools.partial(_pool_kernel, seq_len=seq_len)
    return pl.pallas_call(
        kernel,
        out_shape=jax.ShapeDtypeStruct(x.shape, F32),
        grid=(b, nt),
        in_specs=[
            pl.BlockSpec((1, tm, d), lambda i, t: (i, t, 0)),
            pl.BlockSpec((1, HALO, d), lambda i, t: (i, jnp.maximum(t * halo_blocks - 1, 0), 0)),
            pl.BlockSpec((1, HALO, d), lambda i, t: (i, jnp.minimum((t + 1) * halo_blocks, last_halo_block), 0)),
            pl.BlockSpec((1, 3, d), lambda i, t: (mod_row(i), 0, 0)),
            _resident((1, d)),
            _resident((1, d)),
            _resident(w_in.shape),
            _resident(w_grp.shape),
            _resident((1, width)),
            _resident((1, width)),
            _resident(w_out.shape),
        ],
        out_specs=pl.BlockSpec((1, tm, d), lambda i, t: (i, t, 0)),
        scratch_shapes=[pltpu.VMEM((tm, width), BF16)],
        compiler_params=_params(2),
        name="pool_layer",
    )(x, x, x, mod, pre, post, w_in, w_grp, b_grp, p_scale, w_out)


def _mla_proj_kernel(ctx_ref, x_ref, mod_ref, pre_ref, t1_ref, t2_ref, t1t_ref, wa_ref, wg_ref, wuqt_ref,
                     wuk_ref, wuvt_ref, qn_ref, kvn_ref, q_ref, sg_ref, kn_ref, vt_ref, kr_ref):
    t = pl.program_id(1)
    mod = mod_ref[0]
    gain = pre_ref[...] * (1.0 + mod[1:2])
    xv = jnp.where(t == 0, ctx_ref[0], x_ref[0])
    hb = (xv * _rms_scale(xv) * gain + mod[0:1]).astype(BF16)

    pa = jnp.dot(hb, wa_ref[...], preferred_element_type=F32)
    c_q = pa[:, :Q_LORA_RANK]
    c_kv = pa[:, Q_LORA_RANK:Q_LORA_RANK + KV_LORA_RANK]
    kr = pa[:, Q_LORA_RANK + KV_LORA_RANK:]

    ckv_n = (c_kv * _rms_scale(c_kv) * kvn_ref[...]).astype(BF16)
    kn_ref[0] = jnp.dot(ckv_n, wuk_ref[...], preferred_element_type=F32).astype(BF16)
    vt_ref[0] = lax.dot_general(wuvt_ref[...], ckv_n, (((1,), (1,)), ((), ())),
                                preferred_element_type=F32).astype(BF16)
    kr_ref[0] = (kr * t1_ref[...] + pltpu.roll(kr, QK_ROPE_DIM, axis=1) * t2_ref[...]).astype(BF16)

    @pl.when(t > 0)
    def _():
        cq_n = (c_q * _rms_scale(c_q) * (qn_ref[...] * (MLA_SCALE * LOG2_E))).astype(BF16)
        zt = lax.dot_general(wuqt_ref[...], cq_n, (((1,), (1,)), ((), ())), preferred_element_type=F32)
        t1t = t1t_ref[...]
        hd = 2 * V7X_LANES
        for h in range(MLA_HEADS):
            q_ref[0, h * hd:h * hd + V7X_LANES, :] = zt[h * hd:h * hd + V7X_LANES, :].astype(BF16)
            q_ref[0, h * hd + V7X_LANES:(h + 1) * hd, :] = (zt[h * hd + V7X_LANES:(h + 1) * hd, :] * t1t).astype(BF16)
        sg_ref[0] = _silu(jnp.dot(hb, wg_ref[...], preferred_element_type=F32)).astype(BF16)


def _mla_project(x, ctx, mod, pre, t1, t2, w_a, w_g, w_uqt, w_uk, w_uvt, q_norm, kv_norm):
    b, seq_len, d = x.shape
    tm = ROW_TILE
    assert ctx.shape[1] == tm
    nt = seq_len // tm + 1
    lk = seq_len + tm
    ctx_row = b
    lat = lambda i, t: (i, jnp.maximum(t - 1, 0), 0)
    lat_t = lambda i, t: (i, 0, jnp.maximum(t - 1, 0))
    return pl.pallas_call(
        _mla_proj_kernel,
        out_shape=(
            jax.ShapeDtypeStruct((b, w_uqt.shape[0], seq_len), BF16),
            jax.ShapeDtypeStruct((b, seq_len, w_g.shape[1]), BF16),
            jax.ShapeDtypeStruct((b, lk, w_uk.shape[1]), BF16),
            jax.ShapeDtypeStruct((b, w_uvt.shape[0], lk), BF16),
            jax.ShapeDtypeStruct((b, lk, V7X_LANES), BF16),
        ),
        grid=(b, nt),
        in_specs=[
            pl.BlockSpec((1, tm, d), lambda i, t: (i, 0, 0)),
            pl.BlockSpec((1, tm, d), lat),
            pl.BlockSpec((1, 3, d), lambda i, t: (jnp.where(t == 0, ctx_row, i), 0, 0)),
            _resident((1, d)),
            pl.BlockSpec((tm, V7X_LANES), lambda i, t: (t, 0)),
            pl.BlockSpec((tm, V7X_LANES), lambda i, t: (t, 0)),
            pl.BlockSpec((V7X_LANES, tm), lambda i, t: (0, t)),
            _resident(w_a.shape),
            _resident(w_g.shape),
            _resident(w_uqt.shape),
            _resident(w_uk.shape),
            _resident(w_uvt.shape),
            _resident((1, Q_LORA_RANK)),
            _resident((1, KV_LORA_RANK)),
        ],
        out_specs=(
            pl.BlockSpec((1, w_uqt.shape[0], tm), lat_t),
            pl.BlockSpec((1, tm, w_g.shape[1]), lat),
            pl.BlockSpec((1, tm, w_uk.shape[1]), lambda i, t: (i, t, 0)),
            pl.BlockSpec((1, w_uvt.shape[0], tm), lambda i, t: (i, 0, t)),
            pl.BlockSpec((1, tm, V7X_LANES), lambda i, t: (i, t, 0)),
        ),
        compiler_params=_params(2),
        name="mla_project",
    )(ctx, x, mod, pre, t1, t2, t1.T, w_a, w_g, w_uqt, w_uk, w_uvt, q_norm, kv_norm)


def _attn_kernel(qt_ref, kn_ref, kr_ref, vt_ref, o_ref, kcat_scr, vaug_scr, s_scr, p_scr, acc_scr):
    dv = vt_ref.shape[1]
    lk = kcat_scr.shape[0]
    kb, tq = s_scr.shape[1], s_scr.shape[2]
    n_kb = lk // kb
    n_items = (qt_ref.shape[2] // tq) * n_kb

    kcat_scr[:, :V7X_LANES] = kn_ref[0]
    kcat_scr[:, V7X_LANES:] = kr_ref[0]
    vaug_scr[:dv, :] = vt_ref[0]
    vaug_scr[dv:, :] = jnp.ones((vaug_scr.shape[0] - dv, lk), BF16)

    run_max = None
    alpha = {}
    for i in range(n_items + 2):
        if i < n_items:
            j, c = divmod(i, n_kb)
            s_scr[i % 2] = jnp.dot(kcat_scr[c * kb:(c + 1) * kb, :], qt_ref[0, :, j * tq:(j + 1) * tq],
                                   preferred_element_type=F32)
        if 1 <= i <= n_items:
            g = i - 1
            blk_max = jnp.max(s_scr[g % 2], axis=0, keepdims=True)
            if g % n_kb == 0:
                new_max = blk_max
            else:
                new_max = jnp.maximum(run_max, blk_max)
                alpha[g] = jnp.exp2(run_max - new_max)
            p_scr[g % 2] = jnp.exp2(s_scr[g % 2] - new_max).astype(BF16)
            run_max = new_max
        if i >= 2:
            g = i - 2
            j, c = divmod(g, n_kb)
            pv = jnp.dot(vaug_scr[:, c * kb:(c + 1) * kb], p_scr[g % 2], preferred_element_type=F32)
            if c == 0:
                acc_scr[j % 2] = pv
            else:
                acc_scr[j % 2] = acc_scr[j % 2] * alpha.pop(g) + pv
            if c == n_kb - 1:
                acc = acc_scr[j % 2]
                o_ref[0, j * tq:(j + 1) * tq, :] = (acc[:dv] * (1.0 / acc[dv:dv + 1])).T.astype(BF16)


def _attention(qt, kn, kr, vt):
    b, _, seq_len = qt.shape
    lk = kn.shape[1]
    assert lk % KEY_BLOCK == 0 and seq_len % Q_SUBTILE == 0
    v_rows = V_HEAD_DIM + 2 * V7X_SUBLANES
    return pl.pallas_call(
        _attn_kernel,
        out_shape=jax.ShapeDtypeStruct((b, seq_len, MLA_HEADS * V_HEAD_DIM), BF16),
        grid=(b, MLA_HEADS),
        in_specs=[
            pl.BlockSpec((1, 2 * V7X_LANES, seq_len), lambda i, h: (i, h, 0)),
            pl.BlockSpec((1, lk, QK_NOPE_DIM), lambda i, h: (i, 0, h)),
            pl.BlockSpec((1, lk, V7X_LANES), lambda i, h: (i, 0, 0)),
            pl.BlockSpec((1, V_HEAD_DIM, lk), lambda i, h: (i, h, 0)),
        ],
        out_specs=pl.BlockSpec((1, seq_len, V_HEAD_DIM), lambda i, h: (i, 0, h)),
        scratch_shapes=[
            pltpu.VMEM((lk, 2 * V7X_LANES), BF16),
            pltpu.VMEM((v_rows, lk), BF16),
            pltpu.VMEM((2, KEY_BLOCK, Q_SUBTILE), F32),
            pltpu.VMEM((2, KEY_BLOCK, Q_SUBTILE), BF16),
            pltpu.VMEM((2, v_rows, Q_SUBTILE), F32),
        ],
        compiler_params=_params(2),
        name="mla_attention",
    )(qt, kn, kr, vt)


def _mla_out_kernel(o_ref, sg_ref, x_ref, mod_ref, post_ref, wout_ref, out_ref):
    y = (o_ref[0].astype(F32) * sg_ref[0].astype(F32)).astype(BF16)
    out = jnp.dot(y, wout_ref[...], preferred_element_type=F32)
    out_ref[0] = x_ref[0] + mod_ref[0][2:3] * (out * _rms_scale(out) * post_ref[...])


def _mla_output(o, sg, x, mod, post, w_out):
    b, seq_len, d = x.shape
    tm = ROW_TILE
    width = o.shape[2]
    return pl.pallas_call(
        _mla_out_kernel,
        out_shape=jax.ShapeDtypeStruct(x.shape, F32),
        grid=(b, seq_len // tm),
        in_specs=[
            pl.BlockSpec((1, tm, width), lambda i, t: (i, t, 0)),
            pl.BlockSpec((1, tm, width), lambda i, t: (i, t, 0)),
            pl.BlockSpec((1, tm, d), lambda i, t: (i, t, 0)),
            pl.BlockSpec((1, 3, d), lambda i, t: (i, 0, 0)),
            _resident((1, d)),
            _resident(w_out.shape),
        ],
        out_specs=pl.BlockSpec((1, tm, d), lambda i, t: (i, t, 0)),
        compiler_params=_params(2),
        name="mla_output",
    )(o, sg, x, mod, post, w_out)


def _swap_rope_halves(w):
    n = QK_ROPE_DIM // 4
    return w.reshape(w.shape[:-1] + (2, 2, n))[..., ::-1, :].reshape(w.shape)


def _rope_tables(seq_len, ctx_len):
    n = QK_ROPE_DIM // 4
    idx = jnp.arange(seq_len, dtype=jnp.int32)
    row = (idx // GRID_W).astype(F32)
    col = (idx % GRID_W).astype(F32)
    freqs = ROPE_BASE ** (-jnp.arange(n, dtype=F32) / n)
    ar, ac = row[:, None] * freqs, col[:, None] * freqs
    cos = jnp.concatenate([jnp.cos(ar), jnp.cos(ar), jnp.cos(ac), jnp.cos(ac)], axis=-1)
    sin = jnp.concatenate([-jnp.sin(ar), jnp.sin(ar), -jnp.sin(ac), jnp.sin(ac)], axis=-1)
    one = jnp.ones((ctx_len, QK_ROPE_DIM), F32)
    zero = jnp.zeros((ctx_len, QK_ROPE_DIM), F32)
    t1 = jnp.concatenate([jnp.concatenate([one, zero], -1), jnp.concatenate([cos, sin], -1)], 0)
    t2 = jnp.concatenate([jnp.concatenate([zero, one], -1), jnp.concatenate([sin, cos], -1)], 0)
    return t1, t2


def kernel(x, c, ctx, c_ctx, ada_w, ada_b, pre_norm, post_norm, pool_w_in, pool_w_grp, pool_b_grp,
           pool_scale, pool_w_out, mla_w_in, mla_q_norm, mla_kv_norm, mla_w_uq, mla_w_ukv, mla_w_out):
    b, seq_len, d = x.shape
    ctx_len = ctx.shape[1]
    assert b + 1 <= MOD_ROWS

    cc = jnp.concatenate([c, c_ctx[None, :], jnp.zeros((MOD_ROWS - b - 1, d), F32)], axis=0)
    mod = _modulation(cc, ada_w, ada_b).reshape(ada_w.shape[0], MOD_ROWS, 3, d)

    width = pool_w_out.shape[1]
    pool_args = (pre_norm[0].reshape(1, d), post_norm[0].reshape(1, d), pool_w_in[0].astype(BF16),
                 pool_w_grp[0].astype(BF16), pool_b_grp[0].reshape(1, width),
                 pool_scale[0].reshape(1, width), pool_w_out[0].astype(BF16))
    x1 = _pool_layer(x, mod[0], lambda i: i, *pool_args)
    ctx1 = _pool_layer(ctx, mod[0], lambda i: b, *pool_args)

    w_in = mla_w_in[0]
    r0 = Q_LORA_RANK + KV_LORA_RANK
    w_kr = w_in[:, r0:r0 + QK_ROPE_DIM]
    w_a = jnp.concatenate([w_in[:, :r0], w_kr, _swap_rope_halves(w_kr)], axis=1).astype(BF16)
    w_g = w_in[:, r0 + QK_ROPE_DIM:].astype(BF16)
    w_uq = mla_w_uq[0].reshape(Q_LORA_RANK, MLA_HEADS, QK_NOPE_DIM + QK_ROPE_DIM)
    w_uq_rope = w_uq[..., QK_NOPE_DIM:]
    w_uq = jnp.concatenate([w_uq[..., :QK_NOPE_DIM], w_uq_rope, _swap_rope_halves(w_uq_rope)], axis=-1)
    w_uqt = w_uq.reshape(Q_LORA_RANK, -1).T.astype(BF16)
    w_ukv = mla_w_ukv[0].reshape(KV_LORA_RANK, MLA_HEADS, QK_NOPE_DIM + V_HEAD_DIM)
    w_uk = w_ukv[..., :QK_NOPE_DIM].reshape(KV_LORA_RANK, -1).astype(BF16)
    w_uvt = w_ukv[..., QK_NOPE_DIM:].reshape(KV_LORA_RANK, -1).T.astype(BF16)
    t1, t2 = _rope_tables(seq_len, ctx_len)

    qt, sg, kn, vt, kr = _mla_project(x1, ctx1, mod[1], pre_norm[1].reshape(1, d), t1, t2, w_a, w_g, w_uqt,
                                      w_uk, w_uvt, mla_q_norm[0].reshape(1, -1), mla_kv_norm[0].reshape(1, -1))
    o = _attention(qt, kn, kr, vt)
    return _mla_output(o, sg, x1, mod[1], post_norm[1].reshape(1, d), mla_w_out[0].astype(BF16))
```

```python
import functools

import jax
import jax.numpy as jnp
import numpy as np
from jax import lax
from jax.experimental import pallas as pl
from jax.experimental.pallas import tpu as pltpu

GRID_W = 64
POOL_GROUPS = 4
POOL_WINDOWS = (2, 4, 8, 16)
MLA_HEADS = 16
Q_LORA_RANK = 512
KV_LORA_RANK = 512
QK_NOPE_DIM = 128
QK_ROPE_DIM = 64
V_HEAD_DIM = 128
MLA_SCALE = (QK_NOPE_DIM + QK_ROPE_DIM) ** -0.5
ROPE_BASE = 10000.0
NORM_EPS = 1e-6

V7X_LANES = 128
V7X_SUBLANES = 8
V7X_VMEM_LIMIT_BYTES = 56 * 1024 * 1024

MOD_ROWS = 16
HALO = V7X_SUBLANES
ROW_TILE = 256
POOL_ROW_TILE = 512
OUT_ROW_TILE = 512
Q_SUBTILE = 512
KEY_BLOCK = 768
MOD_COL_TILE = 1024
LOG2_E = 1.4426950408889634

BF16 = jnp.bfloat16
F32 = jnp.float32


def _silu(v):
    return v * (1.0 / (1.0 + jnp.exp(-v)))


def _rms_scale(v):
    return lax.rsqrt(jnp.mean(v * v, axis=-1, keepdims=True) + NORM_EPS)


def _resident(shape):
    nd = len(shape)
    return pl.BlockSpec(shape, lambda *_: (0,) * nd, pipeline_mode=pl.Buffered(1))


def _params(n_grid):
    return pltpu.CompilerParams(
        dimension_semantics=("arbitrary",) * n_grid,
        vmem_limit_bytes=V7X_VMEM_LIMIT_BYTES)


def _mod_kernel(cc_ref, w_ref, b_ref, o_ref):
    a = _silu(cc_ref[...]).astype(BF16)
    o_ref[0] = jnp.dot(a, w_ref[0].astype(BF16), preferred_element_type=F32) + b_ref[0]


def _modulation(cc, ada_w, ada_b):
    depth, d, n = ada_w.shape
    return pl.pallas_call(
        _mod_kernel,
        out_shape=jax.ShapeDtypeStruct((depth, MOD_ROWS, n), F32),
        grid=(depth, n // MOD_COL_TILE),
        in_specs=[
            pl.BlockSpec((MOD_ROWS, d), lambda i, j: (0, 0)),
            pl.BlockSpec((1, d, MOD_COL_TILE), lambda i, j: (i, 0, j)),
            pl.BlockSpec((1, 1, MOD_COL_TILE), lambda i, j: (i, 0, j)),
        ],
        out_specs=pl.BlockSpec((1, MOD_ROWS, MOD_COL_TILE), lambda i, j: (i, 0, j)),
        compiler_params=_params(2),
        name="adaln_mod",
    )(cc, ada_w, ada_b.reshape(depth, 1, n))


def _pool_kernel(x_ref, xprev_ref, xnext_ref, mod_ref, pre_ref, post_ref, win_ref, wgrp_ref,
                 bgrp_ref, pscale_ref, wout_ref, o_ref, y_scr, *, seq_len):
    t = pl.program_id(1)
    nt = pl.num_programs(1)
    tm = x_ref.shape[1]
    width = wout_ref.shape[0]
    gdim = width // POOL_GROUPS
    n_ext = tm + 2 * HALO

    mod = mod_ref[0]
    shift, gate = mod[0:1], mod[2:3]
    gain = pre_ref[...] * (1.0 + mod[1:2])

    def modulate(v):
        return v * _rms_scale(v) * gain + shift

    x_main = x_ref[0]
    h_main = modulate(x_main).astype(BF16)
    halo = modulate(jnp.concatenate([xnext_ref[0], xprev_ref[0]], axis=0))
    row = lax.broadcasted_iota(jnp.int32, (2 * HALO, 1), 0)
    first_ok = jnp.where(t < nt - 1, 0, HALO)
    end_ok = jnp.where(t > 0, 2 * HALO, HALO)
    halo = jnp.where((row >= first_ok) & (row < end_ok), halo, 0.0).astype(BF16)
    h_ext = jnp.concatenate([h_main, halo], axis=0)

    pos = t * tm + lax.broadcasted_iota(jnp.int32, (tm, 1), 0)
    for g in range(POOL_GROUPS):
        cols = slice(g * gdim, (g + 1) * gdim)
        window = POOL_WINDOWS[g]
        left = window // 2
        right = window - 1 - left
        u = jnp.dot(h_ext, win_ref[:, cols], preferred_element_type=F32)
        acc = u
        span = 1
        while span < window:
            acc = acc + pltpu.roll(acc, span, axis=0)
            span *= 2
        if right:
            acc = pltpu.roll(acc, n_ext - right, axis=0)
        count = jnp.minimum(pos + right + 1, seq_len) - jnp.maximum(pos - left, 0)
        pooled = acc[:tm] * (1.0 / count.astype(F32)) - u[:tm]
        mixed = jnp.dot(pooled.astype(BF16), wgrp_ref[g], preferred_element_type=F32) + bgrp_ref[:, cols]
        gt = jnp.dot(h_main, win_ref[:, width + g * gdim: width + (g + 1) * gdim],
                     preferred_element_type=F32)
        y_scr[:, cols] = (mixed * pscale_ref[:, cols] * _silu(gt)).astype(BF16)

    out = jnp.dot(y_scr[...], wout_ref[...], preferred_element_type=F32)
    o_ref[0] = x_main + gate * (out * _rms_scale(out) * post_ref[...])


def _pool_layer(x, mod, mod_row, pre, post, w_in, w_grp, b_grp, p_scale, w_out):
    b, seq_len, d = x.shape
    tm = min(POOL_ROW_TILE, seq_len)
    nt = seq_len // tm
    width = w_out.shape[0]
    halo_blocks = tm // HALO
    last_halo_block = seq_len // HALO - 1
    kernel = functools.partial(_pool_kernel, seq_len=seq_len)
    return pl.pallas_call(
        kernel,
        out_shape=jax.ShapeDtypeStruct(x.shape, F32),
        grid=(b, nt),
        in_specs=[
            pl.BlockSpec((1, tm, d), lambda i, t: (i, t, 0)),
            pl.BlockSpec((1, HALO, d), lambda i, t: (i, jnp.maximum(t * halo_blocks - 1, 0), 0)),
            pl.BlockSpec((1, HALO, d), lambda i, t: (i, jnp.minimum((t + 1) * halo_blocks, last_halo_block), 0)),
            pl.BlockSpec((1, 3, d), lambda i, t: (mod_row(i), 0, 0)),
            _resident((1, d)),
            _resident((1, d)),
            _resident(w_in.shape),
            _resident(w_grp.shape),
            _resident((1, width)),
            _resident((1, width)),
            _resident(w_out.shape),
        ],
        out_specs=pl.BlockSpec((1, tm, d), lambda i, t: (i, t, 0)),
        scratch_shapes=[pltpu.VMEM((tm, width), BF16)],
        compiler_params=_params(2),
        name="pool_layer",
    )(x, x, x, mod, pre, post, w_in, w_grp, b_grp, p_scale, w_out)


def _mla_proj_kernel(ctx_ref, x_ref, mod_ref, pre_ref, t1_ref, t2_ref, t1t_ref, wa_ref, wg_ref, wuqt_ref,
                     wuk_ref, wuvt_ref, qn_ref, kvn_ref, q_ref, sg_ref, kn_ref, vt_ref, kr_ref):
    t = pl.program_id(1)
    mod = mod_ref[0]
    gain = pre_ref[...] * (1.0 + mod[1:2])
    xv = jnp.where(t == 0, ctx_ref[0], x_ref[0])
    hb = (xv * _rms_scale(xv) * gain + mod[0:1]).astype(BF16)

    pa = jnp.dot(hb, wa_ref[...], preferred_element_type=F32)
    c_q = pa[:, :Q_LORA_RANK]
    c_kv = pa[:, Q_LORA_RANK:Q_LORA_RANK + KV_LORA_RANK]
    kr = pa[:, Q_LORA_RANK + KV_LORA_RANK:]

    ckv_n = (c_kv * _rms_scale(c_kv) * kvn_ref[...]).astype(BF16)
    kn_ref[0] = jnp.dot(ckv_n, wuk_ref[...], preferred_element_type=F32).astype(BF16)
    vt_ref[0] = lax.dot_general(wuvt_ref[...], ckv_n, (((1,), (1,)), ((), ())),
                                preferred_element_type=F32).astype(BF16)
    kr_ref[0] = (kr * t1_ref[...] + pltpu.roll(kr, QK_ROPE_DIM, axis=1) * t2_ref[...]).astype(BF16)

    @pl.when(t > 0)
    def _():
        cq_n = (c_q * _rms_scale(c_q) * (qn_ref[...] * (MLA_SCALE * LOG2_E))).astype(BF16)
        zt = lax.dot_general(wuqt_ref[...], cq_n, (((1,), (1,)), ((), ())), preferred_element_type=F32)
        t1t = t1t_ref[...]
        hd = 2 * V7X_LANES
        for h in range(MLA_HEADS):
            q_ref[0, h * hd:h * hd + V7X_LANES, :] = zt[h * hd:h * hd + V7X_LANES, :].astype(BF16)
            q_ref[0, h * hd + V7X_LANES:(h + 1) * hd, :] = (zt[h * hd + V7X_LANES:(h + 1) * hd, :] * t1t).astype(BF16)
        sg_ref[0] = _silu(jnp.dot(hb, wg_ref[...], preferred_element_type=F32)).astype(BF16)


def _mla_project(x, ctx, mod, pre, t1, t2, w_a, w_g, w_uqt, w_uk, w_uvt, q_norm, kv_norm):
    b, seq_len, d = x.shape
    tm = ROW_TILE
    assert ctx.shape[1] == tm
    nt = seq_len // tm + 1
    lk = seq_len + tm
    ctx_row = b
    lat = lambda i, t: (i, jnp.maximum(t - 1, 0), 0)
    lat_t = lambda i, t: (i, 0, jnp.maximum(t - 1, 0))
    return pl.pallas_call(
        _mla_proj_kernel,
        out_shape=(
            jax.ShapeDtypeStruct((b, w_uqt.shape[0], seq_len), BF16),
            jax.ShapeDtypeStruct((b, seq_len, w_g.shape[1]), BF16),
            jax.ShapeDtypeStruct((b, lk, w_uk.shape[1]), BF16),
            jax.ShapeDtypeStruct((b, w_uvt.shape[0], lk), BF16),
            jax.ShapeDtypeStruct((b, lk, V7X_LANES), BF16),
        ),
        grid=(b, nt),
        in_specs=[
            pl.BlockSpec((1, tm, d), lambda i, t: (i, 0, 0)),
            pl.BlockSpec((1, tm, d), lat),
            pl.BlockSpec((1, 3, d), lambda i, t: (jnp.where(t == 0, ctx_row, i), 0, 0)),
            _resident((1, d)),
            pl.BlockSpec((tm, V7X_LANES), lambda i, t: (t, 0)),
            pl.BlockSpec((tm, V7X_LANES), lambda i, t: (t, 0)),
            pl.BlockSpec((V7X_LANES, tm), lambda i, t: (0, t)),
            _resident(w_a.shape),
            _resident(w_g.shape),
            _resident(w_uqt.shape),
            _resident(w_uk.shape),
            _resident(w_uvt.shape),
            _resident((1, Q_LORA_RANK)),
            _resident((1, KV_LORA_RANK)),
        ],
        out_specs=(
            pl.BlockSpec((1, w_uqt.shape[0], tm), lat_t),
            pl.BlockSpec((1, tm, w_g.shape[1]), lat),
            pl.BlockSpec((1, tm, w_uk.shape[1]), lambda i, t: (i, t, 0)),
            pl.BlockSpec((1, w_uvt.shape[0], tm), lambda i, t: (i, 0, t)),
            pl.BlockSpec((1, tm, V7X_LANES), lambda i, t: (i, t, 0)),
        ),
        compiler_params=_params(2),
        name="mla_project",
    )(ctx, x, mod, pre, t1, t2, np.ascontiguousarray(t1.T), w_a, w_g, w_uqt, w_uk, w_uvt, q_norm, kv_norm)


def _attn_kernel(qt_ref, kn_ref, kr_ref, vt_ref, o_ref, kcat_scr, vaug_scr, s_scr, p_scr, acc_scr):
    dv = vt_ref.shape[1]
    lk = kcat_scr.shape[0]
    kb, tq = s_scr.shape[1], s_scr.shape[2]
    n_kb = lk // kb
    n_items = (qt_ref.shape[2] // tq) * n_kb

    kcat_scr[:, :V7X_LANES] = kn_ref[0]
    kcat_scr[:, V7X_LANES:] = kr_ref[0]
    vaug_scr[:dv, :] = vt_ref[0]
    vaug_scr[dv:, :] = jnp.ones((vaug_scr.shape[0] - dv, lk), BF16)

    run_max = None
    alpha = {}
    for i in range(n_items + 2):
        if i < n_items:
            j, c = divmod(i, n_kb)
            s_scr[i % 2] = jnp.dot(kcat_scr[c * kb:(c + 1) * kb, :], qt_ref[0, :, j * tq:(j + 1) * tq],
                                   preferred_element_type=F32)
        if 1 <= i <= n_items:
            g = i - 1
            blk_max = jnp.max(s_scr[g % 2], axis=0, keepdims=True)
            if g % n_kb == 0:
                new_max = blk_max
            else:
                new_max = jnp.maximum(run_max, blk_max)
                alpha[g] = jnp.exp2(run_max - new_max)
            p_scr[g % 2] = jnp.exp2(s_scr[g % 2] - new_max).astype(BF16)
            run_max = new_max
        if i >= 2:
            g = i - 2
            j, c = divmod(g, n_kb)
            pv = jnp.dot(vaug_scr[:, c * kb:(c + 1) * kb], p_scr[g % 2], preferred_element_type=F32)
            if c == 0:
                acc_scr[j % 2] = pv
            else:
                acc_scr[j % 2] = acc_scr[j % 2] * alpha.pop(g) + pv
            if c == n_kb - 1:
                acc = acc_scr[j % 2]
                o_ref[0, j * tq:(j + 1) * tq, :] = (acc[:dv] * (1.0 / acc[dv:dv + 1])).T.astype(BF16)


def _attention(qt, kn, kr, vt):
    b, _, seq_len = qt.shape
    lk = kn.shape[1]
    assert lk % KEY_BLOCK == 0 and seq_len % Q_SUBTILE == 0
    v_rows = V_HEAD_DIM + 2 * V7X_SUBLANES
    return pl.pallas_call(
        _attn_kernel,
        out_shape=jax.ShapeDtypeStruct((b, seq_len, MLA_HEADS * V_HEAD_DIM), BF16),
        grid=(b, MLA_HEADS),
        in_specs=[
            pl.BlockSpec((1, 2 * V7X_LANES, seq_len), lambda i, h: (i, h, 0)),
            pl.BlockSpec((1, lk, QK_NOPE_DIM), lambda i, h: (i, 0, h)),
            pl.BlockSpec((1, lk, V7X_LANES), lambda i, h: (i, 0, 0)),
            pl.BlockSpec((1, V_HEAD_DIM, lk), lambda i, h: (i, h, 0)),
        ],
        out_specs=pl.BlockSpec((1, seq_len, V_HEAD_DIM), lambda i, h: (i, 0, h)),
        scratch_shapes=[
            pltpu.VMEM((lk, 2 * V7X_LANES), BF16),
            pltpu.VMEM((v_rows, lk), BF16),
            pltpu.VMEM((2, KEY_BLOCK, Q_SUBTILE), F32),
            pltpu.VMEM((2, KEY_BLOCK, Q_SUBTILE), BF16),
            pltpu.VMEM((2, v_rows, Q_SUBTILE), F32),
        ],
        compiler_params=_params(2),
        name="mla_attention",
    )(qt, kn, kr, vt)


def _mla_out_kernel(o_ref, sg_ref, x_ref, mod_ref, post_ref, wout_ref, out_ref):
    y = (o_ref[0].astype(F32) * sg_ref[0].astype(F32)).astype(BF16)
    out = jnp.dot(y, wout_ref[...], preferred_element_type=F32)
    out_ref[0] = x_ref[0] + mod_ref[0][2:3] * (out * _rms_scale(out) * post_ref[...])


def _mla_output(o, sg, x, mod, post, w_out):
    b, seq_len, d = x.shape
    tm = OUT_ROW_TILE
    width = o.shape[2]
    return pl.pallas_call(
        _mla_out_kernel,
        out_shape=jax.ShapeDtypeStruct(x.shape, F32),
        grid=(b, seq_len // tm),
        in_specs=[
            pl.BlockSpec((1, tm, width), lambda i, t: (i, t, 0)),
            pl.BlockSpec((1, tm, width), lambda i, t: (i, t, 0)),
            pl.BlockSpec((1, tm, d), lambda i, t: (i, t, 0)),
            pl.BlockSpec((1, 3, d), lambda i, t: (i, 0, 0)),
            _resident((1, d)),
            _resident(w_out.shape),
        ],
        out_specs=pl.BlockSpec((1, tm, d), lambda i, t: (i, t, 0)),
        compiler_params=_params(2),
        name="mla_output",
    )(o, sg, x, mod, post, w_out)


def _swap_rope_halves(w):
    n = QK_ROPE_DIM // 4
    return w.reshape(w.shape[:-1] + (2, 2, n))[..., ::-1, :].reshape(w.shape)


def _rope_tables(seq_len, ctx_len):
    n = QK_ROPE_DIM // 4
    idx = np.arange(seq_len)
    row = (idx // GRID_W).astype(np.float32)
    col = (idx % GRID_W).astype(np.float32)
    freqs = (ROPE_BASE ** (-np.arange(n, dtype=np.float32) / n)).astype(np.float32)
    ar, ac = row[:, None] * freqs, col[:, None] * freqs
    cos = np.concatenate([np.cos(ar), np.cos(ar), np.cos(ac), np.cos(ac)], axis=-1)
    sin = np.concatenate([-np.sin(ar), np.sin(ar), -np.sin(ac), np.sin(ac)], axis=-1)
    one = np.ones((ctx_len, QK_ROPE_DIM), np.float32)
    zero = np.zeros((ctx_len, QK_ROPE_DIM), np.float32)
    t1 = np.concatenate([np.concatenate([one, zero], -1), np.concatenate([cos, sin], -1)], 0)
    t2 = np.concatenate([np.concatenate([zero, one], -1), np.concatenate([sin, cos], -1)], 0)
    return t1.astype(np.float32), t2.astype(np.float32)


def kernel(x, c, ctx, c_ctx, ada_w, ada_b, pre_norm, post_norm, pool_w_in, pool_w_grp, pool_b_grp,
           pool_scale, pool_w_out, mla_w_in, mla_q_norm, mla_kv_norm, mla_w_uq, mla_w_ukv, mla_w_out):
    b, seq_len, d = x.shape
    ctx_len = ctx.shape[1]
    assert b + 1 <= MOD_ROWS

    cc = jnp.concatenate([c, c_ctx[None, :], jnp.zeros((MOD_ROWS - b - 1, d), F32)], axis=0)
    mod = _modulation(cc, ada_w, ada_b).reshape(ada_w.shape[0], MOD_ROWS, 3, d)

    width = pool_w_out.shape[1]
    pool_args = (pre_norm[0].reshape(1, d), post_norm[0].reshape(1, d), pool_w_in[0].astype(BF16),
                 pool_w_grp[0].astype(BF16), pool_b_grp[0].reshape(1, width),
                 pool_scale[0].reshape(1, width), pool_w_out[0].astype(BF16))
    x1 = _pool_layer(x, mod[0], lambda i: i, *pool_args)
    ctx1 = _pool_layer(ctx, mod[0], lambda i: b, *pool_args)

    w_in = mla_w_in[0]
    r0 = Q_LORA_RANK + KV_LORA_RANK
    w_kr = w_in[:, r0:r0 + QK_ROPE_DIM]
    w_a = jnp.concatenate([w_in[:, :r0], w_kr, _swap_rope_halves(w_kr)], axis=1).astype(BF16)
    w_g = w_in[:, r0 + QK_ROPE_DIM:].astype(BF16)
    w_uq = mla_w_uq[0].reshape(Q_LORA_RANK, MLA_HEADS, QK_NOPE_DIM + QK_ROPE_DIM)
    w_uq_rope = w_uq[..., QK_NOPE_DIM:]
    w_uq = jnp.concatenate([w_uq[..., :QK_NOPE_DIM], w_uq_rope, _swap_rope_halves(w_uq_rope)], axis=-1)
    w_uqt = w_uq.reshape(Q_LORA_RANK, -1).T.astype(BF16)
    w_ukv = mla_w_ukv[0].reshape(KV_LORA_RANK, MLA_HEADS, QK_NOPE_DIM + V_HEAD_DIM)
    w_uk = w_ukv[..., :QK_NOPE_DIM].reshape(KV_LORA_RANK, -1).astype(BF16)
    w_uvt = w_ukv[..., QK_NOPE_DIM:].reshape(KV_LORA_RANK, -1).T.astype(BF16)
    t1, t2 = _rope_tables(seq_len, ctx_len)

    qt, sg, kn, vt, kr = _mla_project(x1, ctx1, mod[1], pre_norm[1].reshape(1, d), t1, t2, w_a, w_g, w_uqt,
                                      w_uk, w_uvt, mla_q_norm[0].reshape(1, -1), mla_kv_norm[0].reshape(1, -1))
    o = _attention(qt, kn, kr, vt)
    return _mla_output(o, sg, x1, mod[1], post_norm[1].reshape(1, d), mla_w_out[0].astype(BF16))
```

```python
import functools

import jax
import jax.numpy as jnp
import numpy as np
from jax import lax
from jax.experimental import pallas as pl
from jax.experimental.pallas import tpu as pltpu

GRID_W = 64
POOL_GROUPS = 4
POOL_WINDOWS = (2, 4, 8, 16)
MLA_HEADS = 16
Q_LORA_RANK = 512
KV_LORA_RANK = 512
QK_NOPE_DIM = 128
QK_ROPE_DIM = 64
V_HEAD_DIM = 128
MLA_SCALE = (QK_NOPE_DIM + QK_ROPE_DIM) ** -0.5
ROPE_BASE = 10000.0
NORM_EPS = 1e-6

V7X_LANES = 128
V7X_SUBLANES = 8
V7X_VMEM_LIMIT_BYTES = 56 * 1024 * 1024

MOD_ROWS = 16
HALO = V7X_SUBLANES
ROW_TILE = 256
POOL_ROW_TILE = 512
OUT_ROW_TILE = 512
Q_SUBTILE = 512
FIRST_KEY_BLOCK = 256
KEY_BLOCK = 512
OVERFLOW_MARGIN = 64.0
MOD_COL_TILE = 1024
LOG2_E = 1.4426950408889634

BF16 = jnp.bfloat16
F32 = jnp.float32


def _silu(v):
    return v * (1.0 / (1.0 + jnp.exp(-v)))


def _rms_scale(v):
    return lax.rsqrt(jnp.mean(v * v, axis=-1, keepdims=True) + NORM_EPS)


def _resident(shape):
    nd = len(shape)
    return pl.BlockSpec(shape, lambda *_: (0,) * nd, pipeline_mode=pl.Buffered(1))


def _params(n_grid):
    return pltpu.CompilerParams(
        dimension_semantics=("arbitrary",) * n_grid,
        vmem_limit_bytes=V7X_VMEM_LIMIT_BYTES)


def _mod_kernel(cc_ref, w_ref, b_ref, o_ref):
    a = _silu(cc_ref[...]).astype(BF16)
    o_ref[0] = jnp.dot(a, w_ref[0].astype(BF16), preferred_element_type=F32) + b_ref[0]


def _modulation(cc, ada_w, ada_b):
    depth, d, n = ada_w.shape
    return pl.pallas_call(
        _mod_kernel,
        out_shape=jax.ShapeDtypeStruct((depth, MOD_ROWS, n), F32),
        grid=(depth, n // MOD_COL_TILE),
        in_specs=[
            pl.BlockSpec((MOD_ROWS, d), lambda i, j: (0, 0)),
            pl.BlockSpec((1, d, MOD_COL_TILE), lambda i, j: (i, 0, j)),
            pl.BlockSpec((1, 1, MOD_COL_TILE), lambda i, j: (i, 0, j)),
        ],
        out_specs=pl.BlockSpec((1, MOD_ROWS, MOD_COL_TILE), lambda i, j: (i, 0, j)),
        compiler_params=_params(2),
        name="adaln_mod",
    )(cc, ada_w, ada_b.reshape(depth, 1, n))


def _pool_kernel(x_ref, xprev_ref, xnext_ref, mod_ref, pre_ref, post_ref, win_ref, wgrp_ref,
                 bgrp_ref, pscale_ref, wout_ref, o_ref, y_scr, *, seq_len):
    t = pl.program_id(1)
    nt = pl.num_programs(1)
    tm = x_ref.shape[1]
    width = wout_ref.shape[0]
    gdim = width // POOL_GROUPS
    n_ext = tm + 2 * HALO

    mod = mod_ref[0]
    shift, gate = mod[0:1], mod[2:3]
    gain = pre_ref[...] * (1.0 + mod[1:2])

    def modulate(v):
        return v * _rms_scale(v) * gain + shift

    x_main = x_ref[0]
    h_main = modulate(x_main).astype(BF16)
    halo = modulate(jnp.concatenate([xnext_ref[0], xprev_ref[0]], axis=0))
    row = lax.broadcasted_iota(jnp.int32, (2 * HALO, 1), 0)
    first_ok = jnp.where(t < nt - 1, 0, HALO)
    end_ok = jnp.where(t > 0, 2 * HALO, HALO)
    halo = jnp.where((row >= first_ok) & (row < end_ok), halo, 0.0).astype(BF16)
    h_ext = jnp.concatenate([h_main, halo], axis=0)

    pos = t * tm + lax.broadcasted_iota(jnp.int32, (tm, 1), 0)
    for g in range(POOL_GROUPS):
        cols = slice(g * gdim, (g + 1) * gdim)
        window = POOL_WINDOWS[g]
        left = window // 2
        right = window - 1 - left
        u = jnp.dot(h_ext, win_ref[:, cols], preferred_element_type=F32)
        acc = u
        span = 1
        while span < window:
            acc = acc + pltpu.roll(acc, span, axis=0)
            span *= 2
        if right:
            acc = pltpu.roll(acc, n_ext - right, axis=0)
        count = jnp.minimum(pos + right + 1, seq_len) - jnp.maximum(pos - left, 0)
        pooled = acc[:tm] * (1.0 / count.astype(F32)) - u[:tm]
        mixed = jnp.dot(pooled.astype(BF16), wgrp_ref[g], preferred_element_type=F32) + bgrp_ref[:, cols]
        gt = jnp.dot(h_main, win_ref[:, width + g * gdim: width + (g + 1) * gdim],
                     preferred_element_type=F32)
        y_scr[:, cols] = (mixed * pscale_ref[:, cols] * _silu(gt)).astype(BF16)

    out = jnp.dot(y_scr[...], wout_ref[...], preferred_element_type=F32)
    o_ref[0] = x_main + gate * (out * _rms_scale(out) * post_ref[...])


def _pool_layer(x, mod, mod_row, pre, post, w_in, w_grp, b_grp, p_scale, w_out):
    b, seq_len, d = x.shape
    tm = min(POOL_ROW_TILE, seq_len)
    nt = seq_len // tm
    width = w_out.shape[0]
    halo_blocks = tm // HALO
    last_halo_block = seq_len // HALO - 1
    kernel = functools.partial(_pool_kernel, seq_len=seq_len)
    return pl.pallas_call(
        kernel,
        out_shape=jax.ShapeDtypeStruct(x.shape, F32),
        grid=(b, nt),
        in_specs=[
            pl.BlockSpec((1, tm, d), lambda i, t: (i, t, 0)),
            pl.BlockSpec((1, HALO, d), lambda i, t: (i, jnp.maximum(t * halo_blocks - 1, 0), 0)),
            pl.BlockSpec((1, HALO, d), lambda i, t: (i, jnp.minimum((t + 1) * halo_blocks, last_halo_block), 0)),
            pl.BlockSpec((1, 3, d), lambda i, t: (mod_row(i), 0, 0)),
            _resident((1, d)),
            _resident((1, d)),
            _resident(w_in.shape),
            _resident(w_grp.shape),
            _resident((1, width)),
            _resident((1, width)),
            _resident(w_out.shape),
        ],
        out_specs=pl.BlockSpec((1, tm, d), lambda i, t: (i, t, 0)),
        scratch_shapes=[pltpu.VMEM((tm, width), BF16)],
        compiler_params=_params(2),
        name="pool_layer",
    )(x, x, x, mod, pre, post, w_in, w_grp, b_grp, p_scale, w_out)


def _mla_proj_kernel(ctx_ref, x_ref, mod_ref, pre_ref, t1_ref, t2_ref, t1t_ref, wa_ref, wg_ref, wuqt_ref,
                     wuk_ref, wuvt_ref, qn_ref, kvn_ref, q_ref, sg_ref, kn_ref, vt_ref, kr_ref):
    t = pl.program_id(1)
    mod = mod_ref[0]
    gain = pre_ref[...] * (1.0 + mod[1:2])
    xv = jnp.where(t == 0, ctx_ref[0], x_ref[0])
    hb = (xv * _rms_scale(xv) * gain + mod[0:1]).astype(BF16)

    pa = jnp.dot(hb, wa_ref[...], preferred_element_type=F32)
    c_q = pa[:, :Q_LORA_RANK]
    c_kv = pa[:, Q_LORA_RANK:Q_LORA_RANK + KV_LORA_RANK]
    kr = pa[:, Q_LORA_RANK + KV_LORA_RANK:]

    ckv_n = (c_kv * _rms_scale(c_kv) * kvn_ref[...]).astype(BF16)
    kn_ref[0] = jnp.dot(ckv_n, wuk_ref[...], preferred_element_type=F32).astype(BF16)
    vt_ref[0] = lax.dot_general(wuvt_ref[...], ckv_n, (((1,), (1,)), ((), ())),
                                preferred_element_type=F32).astype(BF16)
    kr_ref[0] = (kr * t1_ref[...] + pltpu.roll(kr, QK_ROPE_DIM, axis=1) * t2_ref[...]).astype(BF16)

    @pl.when(t > 0)
    def _():
        cq_n = (c_q * _rms_scale(c_q) * (qn_ref[...] * (MLA_SCALE * LOG2_E))).astype(BF16)
        zt = lax.dot_general(wuqt_ref[...], cq_n, (((1,), (1,)), ((), ())), preferred_element_type=F32)
        t1t = t1t_ref[...]
        hd = 2 * V7X_LANES
        for h in range(MLA_HEADS):
            q_ref[0, h * hd:h * hd + V7X_LANES, :] = zt[h * hd:h * hd + V7X_LANES, :].astype(BF16)
            q_ref[0, h * hd + V7X_LANES:(h + 1) * hd, :] = (zt[h * hd + V7X_LANES:(h + 1) * hd, :] * t1t).astype(BF16)
        sg_ref[0] = _silu(jnp.dot(hb, wg_ref[...], preferred_element_type=F32)).astype(BF16)


def _mla_project(x, ctx, mod, pre, t1, t2, w_a, w_g, w_uqt, w_uk, w_uvt, q_norm, kv_norm):
    b, seq_len, d = x.shape
    tm = ROW_TILE
    assert ctx.shape[1] == tm
    nt = seq_len // tm + 1
    lk = seq_len + tm
    ctx_row = b
    lat = lambda i, t: (i, jnp.maximum(t - 1, 0), 0)
    lat_t = lambda i, t: (i, 0, jnp.maximum(t - 1, 0))
    return pl.pallas_call(
        _mla_proj_kernel,
        out_shape=(
            jax.ShapeDtypeStruct((b, w_uqt.shape[0], seq_len), BF16),
            jax.ShapeDtypeStruct((b, seq_len, w_g.shape[1]), BF16),
            jax.ShapeDtypeStruct((b, lk, w_uk.shape[1]), BF16),
            jax.ShapeDtypeStruct((b, w_uvt.shape[0], lk), BF16),
            jax.ShapeDtypeStruct((b, lk, V7X_LANES), BF16),
        ),
        grid=(b, nt),
        in_specs=[
            pl.BlockSpec((1, tm, d), lambda i, t: (i, 0, 0)),
            pl.BlockSpec((1, tm, d), lat),
            pl.BlockSpec((1, 3, d), lambda i, t: (jnp.where(t == 0, ctx_row, i), 0, 0)),
            _resident((1, d)),
            pl.BlockSpec((tm, V7X_LANES), lambda i, t: (t, 0)),
            pl.BlockSpec((tm, V7X_LANES), lambda i, t: (t, 0)),
            pl.BlockSpec((V7X_LANES, tm), lambda i, t: (0, t)),
            _resident(w_a.shape),
            _resident(w_g.shape),
            _resident(w_uqt.shape),
            _resident(w_uk.shape),
            _resident(w_uvt.shape),
            _resident((1, Q_LORA_RANK)),
            _resident((1, KV_LORA_RANK)),
        ],
        out_specs=(
            pl.BlockSpec((1, w_uqt.shape[0], tm), lat_t),
            pl.BlockSpec((1, tm, w_g.shape[1]), lat),
            pl.BlockSpec((1, tm, w_uk.shape[1]), lambda i, t: (i, t, 0)),
            pl.BlockSpec((1, w_uvt.shape[0], tm), lambda i, t: (i, 0, t)),
            pl.BlockSpec((1, tm, V7X_LANES), lambda i, t: (i, t, 0)),
        ),
        compiler_params=_params(2),
        name="mla_project",
    )(ctx, x, mod, pre, t1, t2, np.ascontiguousarray(t1.T), w_a, w_g, w_uqt, w_uk, w_uvt, q_norm, kv_norm)


def _attn_kernel(qt_ref, kn_ref, kr_ref, vt_ref, o_ref, kcat_scr, vaug_scr):
    dv = vt_ref.shape[1]
    lk = kcat_scr.shape[0]
    tq = Q_SUBTILE
    n_tiles = qt_ref.shape[2] // tq
    blocks = [(0, FIRST_KEY_BLOCK)] + [(lo, lo + KEY_BLOCK) for lo in range(FIRST_KEY_BLOCK, lk, KEY_BLOCK)]

    kcat_scr[:, :V7X_LANES] = kn_ref[0]
    kcat_scr[:, V7X_LANES:] = kr_ref[0]
    vaug_scr[:dv, :] = vt_ref[0]
    vaug_scr[dv:, :] = jnp.ones((vaug_scr.shape[0] - dv, lk), BF16)

    def finish(j, acc):
        o_ref[0, j * tq:(j + 1) * tq, :] = (acc[:dv] * (1.0 / acc[dv:dv + 1])).T.astype(BF16)

    ref_max, later_max, acc = {}, {}, {}
    excess = None

    def value_stage(j, bi, p):
        nonlocal excess
        lo, hi = blocks[bi]
        pv = jnp.dot(vaug_scr[:, lo:hi], p, preferred_element_type=F32)
        acc[j] = pv if bi == 0 else acc[j] + pv
        if bi == len(blocks) - 1:
            finish(j, acc.pop(j))
            over = later_max.pop(j) - ref_max.pop(j)
            excess = over if excess is None else jnp.maximum(excess, over)

    pending = None
    for j in range(n_tiles):
        for bi, (lo, hi) in enumerate(blocks):
            st = jnp.dot(kcat_scr[lo:hi, :], qt_ref[0, :, j * tq:(j + 1) * tq], preferred_element_type=F32)
            blk_max = jnp.max(st, axis=0, keepdims=True)
            if bi == 0:
                ref_max[j] = blk_max
            else:
                later_max[j] = blk_max if bi == 1 else jnp.maximum(later_max[j], blk_max)
            p = jnp.exp2(st - ref_max[j]).astype(BF16)
            if pending is not None:
                value_stage(*pending)
            pending = (j, bi, p)
    value_stage(*pending)

    @pl.when(jnp.max(excess) > OVERFLOW_MARGIN)
    def _():
        for j in range(n_tiles):
            st = jnp.dot(kcat_scr[...], qt_ref[0, :, j * tq:(j + 1) * tq], preferred_element_type=F32)
            p = jnp.exp2(st - jnp.max(st, axis=0, keepdims=True)).astype(BF16)
            finish(j, jnp.dot(vaug_scr[...], p, preferred_element_type=F32))


def _attention(qt, kn, kr, vt):
    b, _, seq_len = qt.shape
    lk = kn.shape[1]
    assert (lk - FIRST_KEY_BLOCK) % KEY_BLOCK == 0 and seq_len % Q_SUBTILE == 0
    v_rows = V_HEAD_DIM + 2 * V7X_SUBLANES
    return pl.pallas_call(
        _attn_kernel,
        out_shape=jax.ShapeDtypeStruct((b, seq_len, MLA_HEADS * V_HEAD_DIM), BF16),
        grid=(b, MLA_HEADS),
        in_specs=[
            pl.BlockSpec((1, 2 * V7X_LANES, seq_len), lambda i, h: (i, h, 0)),
            pl.BlockSpec((1, lk, QK_NOPE_DIM), lambda i, h: (i, 0, h)),
            pl.BlockSpec((1, lk, V7X_LANES), lambda i, h: (i, 0, 0)),
            pl.BlockSpec((1, V_HEAD_DIM, lk), lambda i, h: (i, h, 0)),
        ],
        out_specs=pl.BlockSpec((1, seq_len, V_HEAD_DIM), lambda i, h: (i, 0, h)),
        scratch_shapes=[
            pltpu.VMEM((lk, 2 * V7X_LANES), BF16),
            pltpu.VMEM((v_rows, lk), BF16),
        ],
        compiler_params=_params(2),
        name="mla_attention",
    )(qt, kn, kr, vt)


def _mla_out_kernel(o_ref, sg_ref, x_ref, mod_ref, post_ref, wout_ref, out_ref):
    y = (o_ref[0].astype(F32) * sg_ref[0].astype(F32)).astype(BF16)
    out = jnp.dot(y, wout_ref[...], preferred_element_type=F32)
    out_ref[0] = x_ref[0] + mod_ref[0][2:3] * (out * _rms_scale(out) * post_ref[...])


def _mla_output(o, sg, x, mod, post, w_out):
    b, seq_len, d = x.shape
    tm = OUT_ROW_TILE
    width = o.shape[2]
    return pl.pallas_call(
        _mla_out_kernel,
        out_shape=jax.ShapeDtypeStruct(x.shape, F32),
        grid=(b, seq_len // tm),
        in_specs=[
            pl.BlockSpec((1, tm, width), lambda i, t: (i, t, 0)),
            pl.BlockSpec((1, tm, width), lambda i, t: (i, t, 0)),
            pl.BlockSpec((1, tm, d), lambda i, t: (i, t, 0)),
            pl.BlockSpec((1, 3, d), lambda i, t: (i, 0, 0)),
            _resident((1, d)),
            _resident(w_out.shape),
        ],
        out_specs=pl.BlockSpec((1, tm, d), lambda i, t: (i, t, 0)),
        compiler_params=_params(2),
        name="mla_output",
    )(o, sg, x, mod, post, w_out)


def _swap_rope_halves(w):
    n = QK_ROPE_DIM // 4
    return w.reshape(w.shape[:-1] + (2, 2, n))[..., ::-1, :].reshape(w.shape)


def _rope_tables(seq_len, ctx_len):
    n = QK_ROPE_DIM // 4
    idx = np.arange(seq_len)
    row = (idx // GRID_W).astype(np.float32)
    col = (idx % GRID_W).astype(np.float32)
    freqs = (ROPE_BASE ** (-np.arange(n, dtype=np.float32) / n)).astype(np.float32)
    ar, ac = row[:, None] * freqs, col[:, None] * freqs
    cos = np.concatenate([np.cos(ar), np.cos(ar), np.cos(ac), np.cos(ac)], axis=-1)
    sin = np.concatenate([-np.sin(ar), np.sin(ar), -np.sin(ac), np.sin(ac)], axis=-1)
    one = np.ones((ctx_len, QK_ROPE_DIM), np.float32)
    zero = np.zeros((ctx_len, QK_ROPE_DIM), np.float32)
    t1 = np.concatenate([np.concatenate([one, zero], -1), np.concatenate([cos, sin], -1)], 0)
    t2 = np.concatenate([np.concatenate([zero, one], -1), np.concatenate([sin, cos], -1)], 0)
    return t1.astype(np.float32), t2.astype(np.float32)


def kernel(x, c, ctx, c_ctx, ada_w, ada_b, pre_norm, post_norm, pool_w_in, pool_w_grp, pool_b_grp,
           pool_scale, pool_w_out, mla_w_in, mla_q_norm, mla_kv_norm, mla_w_uq, mla_w_ukv, mla_w_out):
    b, seq_len, d = x.shape
    ctx_len = ctx.shape[1]
    assert b + 1 <= MOD_ROWS

    cc = jnp.concatenate([c, c_ctx[None, :], jnp.zeros((MOD_ROWS - b - 1, d), F32)], axis=0)
    mod = _modulation(cc, ada_w, ada_b).reshape(ada_w.shape[0], MOD_ROWS, 3, d)

    width = pool_w_out.shape[1]
    pool_args = (pre_norm[0].reshape(1, d), post_norm[0].reshape(1, d), pool_w_in[0].astype(BF16),
                 pool_w_grp[0].astype(BF16), pool_b_grp[0].reshape(1, width),
                 pool_scale[0].reshape(1, width), pool_w_out[0].astype(BF16))
    x1 = _pool_layer(x, mod[0], lambda i: i, *pool_args)
    ctx1 = _pool_layer(ctx, mod[0], lambda i: b, *pool_args)

    w_in = mla_w_in[0]
    r0 = Q_LORA_RANK + KV_LORA_RANK
    w_kr = w_in[:, r0:r0 + QK_ROPE_DIM]
    w_a = jnp.concatenate([w_in[:, :r0], w_kr, _swap_rope_halves(w_kr)], axis=1).astype(BF16)
    w_g = w_in[:, r0 + QK_ROPE_DIM:].astype(BF16)
    w_uq = mla_w_uq[0].reshape(Q_LORA_RANK, MLA_HEADS, QK_NOPE_DIM + QK_ROPE_DIM)
    w_uq_rope = w_uq[..., QK_NOPE_DIM:]
    w_uq = jnp.concatenate([w_uq[..., :QK_NOPE_DIM], w_uq_rope, _swap_rope_halves(w_uq_rope)], axis=-1)
    w_uqt = w_uq.reshape(Q_LORA_RANK, -1).T.astype(BF16)
    w_ukv = mla_w_ukv[0].reshape(KV_LORA_RANK, MLA_HEADS, QK_NOPE_DIM + V_HEAD_DIM)
    w_uk = w_ukv[..., :QK_NOPE_DIM].reshape(KV_LORA_RANK, -1).astype(BF16)
    w_uvt = w_ukv[..., QK_NOPE_DIM:].reshape(KV_LORA_RANK, -1).T.astype(BF16)
    t1, t2 = _rope_tables(seq_len, ctx_len)

    qt, sg, kn, vt, kr = _mla_project(x1, ctx1, mod[1], pre_norm[1].reshape(1, d), t1, t2, w_a, w_g, w_uqt,
                                      w_uk, w_uvt, mla_q_norm[0].reshape(1, -1), mla_kv_norm[0].reshape(1, -1))
    o = _attention(qt, kn, kr, vt)
    return _mla_output(o, sg, x1, mod[1], post_norm[1].reshape(1, d), mla_w_out[0].astype(BF16))
```
